```python
import math
import jax, jax.numpy as jnp
from jax import lax
import numpy as np

D_MODEL = 1024
BATCH = 4
SEQ = 4096
DEPTH = 1
DEC_BATCH = 8
DEC_SEQ = 64
PAST_LEN = 4096

CHUNK = 64
N_META = 16
HEAD_DIM = 64
A_HEADS = 4
B_HEADS = 8
A_WIDTH = A_HEADS * 2 * HEAD_DIM
B_WIDTH = B_HEADS * HEAD_DIM
MIX_WIDTH = A_WIDTH + B_WIDTH
QKV_WIDTH = 3 * MIX_WIDTH
ROT_DIM = HEAD_DIM // 4
ROPE_THETA = 500000.0
Q_BLOCK = 128
N_KEYS = 128
N_EXPERTS = N_KEYS * N_KEYS
PEER_HEADS = 8
PEER_TOPK = 16
D_KEY = 256
TOKEN_BLOCK = 256
EPS = 1e-6
NEG_BIG = -1e30

kernel_name = 'hybrid_diffattn_stickbreak_peer_stream'


def rms_norm(x, g):
    x32 = x.astype(jnp.float32)
    y = x32 * lax.rsqrt(jnp.mean(x32 * x32, axis=-1, keepdims=True) + EPS)
    return (y * g.astype(jnp.float32)).astype(x.dtype)


def partial_rope(x, pos):
    half = ROT_DIM // 2
    inv_freq = ROPE_THETA ** (-jnp.arange(half, dtype=jnp.float32) * 2.0 / ROT_DIM)
    ang = pos.astype(jnp.float32)[:, None] * inv_freq[None, :]
    cos = jnp.cos(ang)[None, :, None, :]
    sin = jnp.sin(ang)[None, :, None, :]
    x32 = x.astype(jnp.float32)
    x1 = x32[..., :half]
    x2 = x32[..., half:ROT_DIM]
    out = jnp.concatenate([x1 * cos - x2 * sin, x2 * cos + x1 * sin, x32[..., ROT_DIM:]], axis=-1)
    return out.astype(x.dtype)


def chunk_index(pos):
    return jnp.where(pos < N_META, 0, (pos - N_META) // CHUNK + 1)


def project_mixers(hn, w_qkv, pos):
    b, t, _ = hn.shape
    qkv = hn @ w_qkv
    aq, ak, av, bq, bk, bv = jnp.split(
        qkv, [A_WIDTH, 2 * A_WIDTH, 3 * A_WIDTH, 3 * A_WIDTH + B_WIDTH, 3 * A_WIDTH + 2 * B_WIDTH], axis=-1)
    aq = partial_rope(aq.reshape(b, t, 2 * A_HEADS, HEAD_DIM), pos).reshape(b, t, A_HEADS, 2, HEAD_DIM)
    ak = partial_rope(ak.reshape(b, t, 2 * A_HEADS, HEAD_DIM), pos).reshape(b, t, A_HEADS, 2 * HEAD_DIM)
    av = av.reshape(b, t, A_HEADS, 2 * HEAD_DIM)
    bq = bq.reshape(b, t, B_HEADS, HEAD_DIM)
    bk = bk.reshape(b, t, B_HEADS, HEAD_DIM)
    bv = bv.reshape(b, t, B_HEADS, HEAD_DIM)
    return aq, ak, av, bq, bk, bv


def diff_lambda_value(lp, lam_init):
    lp = lp.astype(jnp.float32)
    return jnp.exp(jnp.sum(lp[0] * lp[1])) - jnp.exp(jnp.sum(lp[2] * lp[3])) + lam_init


def diff_attn_block(q, k, v, q_pos, k_pos, k_valid, lam):
    kk = k.reshape(k.shape[:3] + (2, HEAD_DIM))
    s = jnp.einsum('bqhmd,bkhmd->bhmqk', q, kk).astype(jnp.float32) * (HEAD_DIM ** -0.5)
    mask = (chunk_index(k_pos)[None, :] <= chunk_index(q_pos)[:, None]) & k_valid[None, :]
    p = jax.nn.softmax(jnp.where(mask, s, NEG_BIG), axis=-1)
    w = p[:, :, 0] - lam * p[:, :, 1]
    return jnp.einsum('bhqk,bkhe->bqhe', w.astype(v.dtype), v)


def stick_block(q, k, v, q_pos, k_pos, k_valid):
    z = jnp.einsum('bqhd,bkhd->bhqk', q, k).astype(jnp.float32) * (HEAD_DIM ** -0.5)
    mask = (k_pos[None, :] < q_pos[:, None]) & k_valid[None, :]
    log_1m = jnp.where(mask, jax.nn.log_sigmoid(-z), 0.0)
    after = lax.cumsum(log_1m, axis=3, reverse=True) - log_1m
    a = jnp.where(mask, jnp.exp(jax.nn.log_sigmoid(z) + after), 0.0)
    return jnp.einsum('bhqk,bkhd->bqhd', a.astype(v.dtype), v)


def sweep_query_blocks(block_fn, q, q_pos):
    b, t = q.shape[:2]
    nb = t // Q_BLOCK
    qb = jnp.moveaxis(q.reshape((b, nb, Q_BLOCK) + q.shape[2:]), 1, 0)
    out = lax.map(lambda args: block_fn(args[0], args[1]), (qb, q_pos.reshape(nb, Q_BLOCK)))
    return jnp.moveaxis(out, 0, 1).reshape((b, t) + out.shape[3:])


def peer_block(xb, w_q, sub_keys, u, v):
    tb = xb.shape[0]
    q = (xb @ w_q).reshape(tb, PEER_HEADS, 2, D_KEY // 2)
    s = jnp.einsum('thpc,pnc->thpn', q, sub_keys).astype(jnp.float32)
    s_top, i_top = lax.top_k(s, PEER_TOPK)
    cand_s = (s_top[:, :, 0, :, None] + s_top[:, :, 1, None, :]).reshape(tb, PEER_HEADS, PEER_TOPK * PEER_TOPK)
    cand_i = (i_top[:, :, 0, :, None] * N_KEYS + i_top[:, :, 1, None, :]).reshape(tb, PEER_HEADS, PEER_TOPK * PEER_TOPK)
    best_s, best_j = lax.top_k(cand_s, PEER_TOPK)
    expert = jnp.take_along_axis(cand_i, best_j, axis=-1)
    gate = jax.nn.softmax(best_s, axis=-1)
    pre = jnp.einsum('thkd,td->thk', u[expert], xb).astype(jnp.float32)
    act = jax.nn.gelu(pre, approximate=False) * gate
    return jnp.einsum('thk,thkd->td', act.astype(v.dtype), v[expert])


def peer(h, w_q, sub_keys, u, v):
    b, t, d = h.shape
    n = b * t
    nb = -(-n // TOKEN_BLOCK)
    flat = jnp.pad(h.reshape(n, d), ((0, nb * TOKEN_BLOCK - n), (0, 0))).reshape(nb, TOKEN_BLOCK, d)
    out = lax.map(lambda xb: peer_block(xb, w_q, sub_keys, u, v), flat)
    return out.reshape(nb * TOKEN_BLOCK, d)[:n].reshape(b, t, d)


def finish_layer(h, a_out, b_out, a_g, b_g, w_o, g_ffn, w_pq, sub_keys, u, v, lam_init):
    b, t, _ = h.shape
    a = rms_norm(a_out, a_g) * (1.0 - lam_init)
    bo = rms_norm(b_out, b_g)
    merged = jnp.concatenate([a.reshape(b, t, A_WIDTH), bo.reshape(b, t, B_WIDTH)], axis=-1)
    h = h + merged @ w_o
    return h + peer(rms_norm(h, g_ffn), w_pq, sub_keys, u, v)


def prompt_layer(h, w_qkv, lam_p, a_g, b_g, w_o, g_attn, g_ffn, w_pq, sub_keys, u, v, lam_init):
    b, t, _ = h.shape
    t_pad = -(-t // Q_BLOCK) * Q_BLOCK
    pos = jnp.arange(t, dtype=jnp.int32)
    pos_pad = jnp.arange(t_pad, dtype=jnp.int32)
    valid = pos_pad < t
    aq, ak, av, bq, bk, bv = project_mixers(rms_norm(h, g_attn), w_qkv, pos)
    pad = lambda a: jnp.pad(a, [(0, 0), (0, t_pad - t)] + [(0, 0)] * (a.ndim - 2))
    aqp, akp, avp, bqp, bkp, bvp = [pad(a) for a in (aq, ak, av, bq, bk, bv)]
    lam = diff_lambda_value(lam_p, lam_init)
    a_out = sweep_query_blocks(lambda qb, qp: diff_attn_block(qb, akp, avp, qp, pos_pad, valid, lam), aqp, pos_pad)[:, :t]
    b_out = sweep_query_blocks(lambda qb, qp: stick_block(qb, bkp, bvp, qp, pos_pad, valid), bqp, pos_pad)[:, :t]
    h = finish_layer(h, a_out, b_out, a_g, b_g, w_o, g_ffn, w_pq, sub_keys, u, v, lam_init)
    return h, (ak, av, bk, bv)


def sample_layer(h, c_ak, c_av, c_bk, c_bv, w_qkv, lam_p, a_g, b_g, w_o, g_attn, g_ffn, w_pq, sub_keys, u, v, lam_init):
    b, s, _ = h.shape
    n_past = c_ak.shape[1]
    q_pos = n_past + jnp.arange(s, dtype=jnp.int32)
    k_pos = jnp.arange(n_past + s, dtype=jnp.int32)
    valid = jnp.ones((n_past + s,), dtype=bool)
    aq, ak, av, bq, bk, bv = project_mixers(rms_norm(h, g_attn), w_qkv, q_pos)
    lam = diff_lambda_value(lam_p, lam_init)
    a_out = diff_attn_block(aq, jnp.concatenate([c_ak, ak], axis=1), jnp.concatenate([c_av, av], axis=1),
                            q_pos, k_pos, valid, lam)
    b_out = stick_block(bq, jnp.concatenate([c_bk, bk], axis=1), jnp.concatenate([c_bv, bv], axis=1),
                        q_pos, k_pos, valid)
    h = finish_layer(h, a_out, b_out, a_g, b_g, w_o, g_ffn, w_pq, sub_keys, u, v, lam_init)
    return h, (ak, av, bk, bv)


def setup_inputs(seed: int = 0) -> dict:
    key = jax.random.key(seed)
    ks = jax.random.split(key, 19)
    nrm = lambda k, shape, scale=1.0: jax.random.normal(k, shape, jnp.float32) * scale
    t_cache = N_META + PAST_LEN
    return {
        'x_prompt': nrm(ks[0], (BATCH, SEQ, D_MODEL)),
        'x_sample': nrm(ks[1], (DEC_BATCH, DEC_SEQ, D_MODEL)),
        'cache_a_k': nrm(ks[2], (DEPTH, DEC_BATCH, t_cache, A_HEADS, 2 * HEAD_DIM)),
        'cache_a_v': nrm(ks[3], (DEPTH, DEC_BATCH, t_cache, A_HEADS, 2 * HEAD_DIM)),
        'cache_b_k': nrm(ks[4], (DEPTH, DEC_BATCH, t_cache, B_HEADS, HEAD_DIM)),
        'cache_b_v': nrm(ks[5], (DEPTH, DEC_BATCH, t_cache, B_HEADS, HEAD_DIM)),
        'meta_tokens': nrm(ks[6], (N_META, D_MODEL)),
        'g_attn': 1.0 + nrm(ks[7], (DEPTH, D_MODEL), 0.02),
        'w_qkv': nrm(ks[8], (DEPTH, D_MODEL, QKV_WIDTH), D_MODEL ** -0.5),
        'diff_lambda': nrm(ks[9], (DEPTH, 4, HEAD_DIM), 0.1),
        'a_norm_g': 1.0 + nrm(ks[10], (DEPTH, 2 * HEAD_DIM), 0.02),
        'b_norm_g': 1.0 + nrm(ks[11], (DEPTH, HEAD_DIM), 0.02),
        'w_o': nrm(ks[12], (DEPTH, MIX_WIDTH, D_MODEL), MIX_WIDTH ** -0.5),
        'g_ffn': 1.0 + nrm(ks[13], (DEPTH, D_MODEL), 0.02),
        'w_peer_q': nrm(ks[14], (DEPTH, D_MODEL, PEER_HEADS * D_KEY), D_MODEL ** -0.5),
        'peer_sub_keys': nrm(ks[15], (DEPTH, 2, N_KEYS, D_KEY // 2), (D_KEY // 2) ** -0.5),
        'peer_u': nrm(ks[16], (DEPTH, N_EXPERTS, D_MODEL), D_MODEL ** -0.5),
        'peer_v': nrm(ks[17], (DEPTH, N_EXPERTS, D_MODEL), PEER_HEADS ** -0.5),
        'g_final': 1.0 + nrm(ks[18], (D_MODEL,), 0.02),
    }


def reference(x_prompt, x_sample, cache_a_k, cache_a_v, cache_b_k, cache_b_v, meta_tokens, g_attn, w_qkv,
              diff_lambda, a_norm_g, b_norm_g, w_o, g_ffn, w_peer_q, peer_sub_keys, peer_u, peer_v, g_final):
    meta = jnp.broadcast_to(meta_tokens[None].astype(x_prompt.dtype), (x_prompt.shape[0], N_META, D_MODEL))
    hp = jnp.concatenate([meta, x_prompt], axis=1)
    hs = x_sample
    pak, pav, pbk, pbv = [], [], [], []
    sak, sav, sbk, sbv = [], [], [], []
    for l in range(DEPTH):
        lam_init = 0.8 - 0.6 * math.exp(-0.3 * l)
        lw = (w_qkv[l], diff_lambda[l], a_norm_g[l], b_norm_g[l], w_o[l], g_attn[l], g_ffn[l],
              w_peer_q[l], peer_sub_keys[l], peer_u[l], peer_v[l])
        hp, (ak, av, bk, bv) = prompt_layer(hp, *lw, lam_init)
        pak.append(ak); pav.append(av); pbk.append(bk); pbv.append(bv)
        hs, (ak, av, bk, bv) = sample_layer(hs, cache_a_k[l], cache_a_v[l], cache_b_k[l], cache_b_v[l], *lw, lam_init)
        sak.append(ak); sav.append(av); sbk.append(bk); sbv.append(bv)
    y_prompt = rms_norm(hp, g_final)[:, N_META:]
    y_sample = rms_norm(hs, g_final)
    return (y_prompt, y_sample, jnp.stack(pak), jnp.stack(pav), jnp.stack(pbk), jnp.stack(pbv),
            jnp.stack(sak), jnp.stack(sav), jnp.stack(sbk), jnp.stack(sbv))
```

```python
import functools
import math

import jax
import jax.numpy as jnp
from jax import lax
from jax.experimental import pallas as pl
from jax.experimental.pallas import tpu as pltpu

F32 = jnp.float32
BF16 = jnp.bfloat16

D_MODEL = 1024
CHUNK = 64
N_META = 16
HEAD_DIM = 64
A_HEADS = 4
B_HEADS = 8
A_WIDTH = A_HEADS * 2 * HEAD_DIM
B_WIDTH = B_HEADS * HEAD_DIM
MIX_WIDTH = A_WIDTH + B_WIDTH
QKV_WIDTH = 3 * MIX_WIDTH
ROT_DIM = HEAD_DIM // 4
ROPE_THETA = 500000.0
N_KEYS = 128
N_EXPERTS = N_KEYS * N_KEYS
PEER_HEADS = 8
PEER_TOPK = 16
D_KEY = 256
EPS = 1e-6
NEG_BIG = -1e30

LANES = 128
ROW_TILE = 256
ATT_BLOCK = 256
TOPK_TOKENS = 128
PEER_TOKENS = 256
PEER_CHUNK = 2048
VMEM_LIMIT = 56 * 1024 * 1024

COL_AQ, COL_AK, COL_AV = 0, 4, 8
COL_BQ, COL_BK, COL_BV = 12, 16, 20


def _dot(a, b):
    return jnp.dot(a, b, preferred_element_type=F32)


def _dot_nt(a, b):
    return lax.dot_general(a, b, (((1,), (1,)), ((), ())), preferred_element_type=F32)


def _params(sem):
    return pltpu.CompilerParams(dimension_semantics=sem, vmem_limit_bytes=VMEM_LIMIT)


def _qkv_body(x_ref, g_ref, w_ref, c_ref, s1_ref, s2_ref, ak_ref, av_ref, bk_ref, bv_ref, bf_ref):
    x = x_ref[...]
    xn = (x * lax.rsqrt(jnp.mean(x * x, axis=-1, keepdims=True) + EPS) * g_ref[...]).astype(BF16)
    cos_t = c_ref[...]
    sin_lo = s1_ref[...]
    sin_hi = s2_ref[...]
    f32_outs = {1: ak_ref, 2: av_ref, 4: bk_ref, 5: bv_ref}
    for grp in range(6):
        for j in range(4):
            cb = grp * 4 + j
            y = _dot(xn, w_ref[:, cb * LANES:(cb + 1) * LANES])
            if grp in (0, 1):
                y = (y * cos_t + pltpu.roll(y, LANES - ROT_DIM // 2, 1) * sin_lo
                     + pltpu.roll(y, ROT_DIM // 2, 1) * sin_hi)
            if grp in f32_outs:
                f32_outs[grp][:, j * LANES:(j + 1) * LANES] = y
            if grp in (0, 3):
                y = y * (HEAD_DIM ** -0.5)
            bf_ref[:, cb * LANES:(cb + 1) * LANES] = y.astype(BF16)


def _rope_tables(pos):
    half = ROT_DIM // 2
    inv_freq = ROPE_THETA ** (-jnp.arange(half, dtype=F32) * 2.0 / ROT_DIM)
    ang = pos.astype(F32)[:, None] * inv_freq[None, :]
    cos, sin = jnp.cos(ang), jnp.sin(ang)
    t = pos.shape[0]
    pad = jnp.zeros((t, HEAD_DIM - ROT_DIM), F32)
    zero = jnp.zeros((t, half), F32)
    cos_t = jnp.concatenate([cos, cos, pad + 1.0], axis=1)
    sin_lo = jnp.concatenate([-sin, zero, pad], axis=1)
    sin_hi = jnp.concatenate([zero, sin, pad], axis=1)
    tile = lambda a: jnp.concatenate([a, a], axis=1)
    return tile(cos_t), tile(sin_lo), tile(sin_hi)


def _qkv_project(x2d, pos, g, w_bf, tm):
    n = x2d.shape[0]
    t = pos.shape[0]
    per = t // tm
    tabs = _rope_tables(pos)
    row = lambda i: (i, 0)
    tab = lambda i: (i % per, 0)
    full = lambda i: (0, 0)
    f32_out = jax.ShapeDtypeStruct((n, A_WIDTH), F32)
    return pl.pallas_call(
        _qkv_body,
        grid=(n // tm,),
        in_specs=[pl.BlockSpec((tm, D_MODEL), row), pl.BlockSpec((1, D_MODEL), full),
                  pl.BlockSpec((D_MODEL, QKV_WIDTH), full),
                  pl.BlockSpec((tm, LANES), tab), pl.BlockSpec((tm, LANES), tab), pl.BlockSpec((tm, LANES), tab)],
        out_specs=[pl.BlockSpec((tm, A_WIDTH), row)] * 4 + [pl.BlockSpec((tm, QKV_WIDTH), row)],
        out_shape=[f32_out] * 4 + [jax.ShapeDtypeStruct((n, QKV_WIDTH), BF16)],
        compiler_params=_params(("parallel",)),
        name="qkv",
    )(x2d, g.reshape(1, D_MODEL), w_bf, *tabs)


def _stack_heads(q):
    lane = lax.broadcasted_iota(jnp.int32, q.shape, 1)
    zero = jnp.zeros_like(q)
    return jnp.concatenate([jnp.where(lane < HEAD_DIM, q, zero), jnp.where(lane >= HEAD_DIM, q, zero)], axis=0)


def _softmax_step(qs, kb, vb, mask, state):
    m, l, acc = state
    s = _dot_nt(qs, kb)
    if mask is not None:
        s = jnp.where(mask, s, NEG_BIG)
    m_new = jnp.maximum(m, jnp.max(s, axis=1, keepdims=True))
    alpha = jnp.exp(m - m_new)
    p = jnp.exp(s - m_new)
    l = alpha * l + jnp.sum(p, axis=1, keepdims=True)
    acc = alpha * acc + _dot(p.astype(BF16), vb)
    return m_new, l, acc


def _diff_finish(state, lam_ref, g_ref, o_ref, lam_init, tq):
    _, l, acc = state
    lp = lam_ref[...]
    lam = (jnp.exp(jnp.sum(lp[0:1] * lp[1:2], axis=1, keepdims=True))
           - jnp.exp(jnp.sum(lp[2:3] * lp[3:4], axis=1, keepdims=True)) + lam_init)
    o = acc[:tq] / l[:tq] - lam * (acc[tq:] / l[tq:])
    o = o * lax.rsqrt(jnp.mean(o * o, axis=1, keepdims=True) + EPS) * g_ref[...] * (1.0 - lam_init)
    o_ref[...] = o.astype(o_ref.dtype)


def _stick_step(qs, kb, vb, tri, mask, state):
    c, acc = state
    z = _dot_nt(qs, kb)
    lm = -(jnp.maximum(z, 0.0) + jnp.log1p(jnp.exp(-jnp.abs(z))))
    if mask is not None:
        lm = jnp.where(mask, lm, 0.0)
    hi = lm.astype(BF16)
    lo = (lm - hi.astype(F32)).astype(BF16)
    cum = _dot(hi, tri) + _dot(lo, tri)
    a = jnp.exp(z + cum + c)
    if mask is not None:
        a = jnp.where(mask, a, 0.0)
    acc = acc + _dot(a.astype(BF16), vb)
    return c + cum[:, 0:1], acc


def _tri(n):
    r = lax.broadcasted_iota(jnp.int32, (n, n), 0)
    c = lax.broadcasted_iota(jnp.int32, (n, n), 1)
    return jnp.where(r >= c, 1.0, 0.0).astype(BF16)


def _stick_finish(state, g_ref, o_ref, tq):
    _, acc = state
    lane = lax.broadcasted_iota(jnp.int32, (tq, LANES), 1)
    first = lane < HEAD_DIM
    o = jnp.where(first, acc[:tq], acc[tq:])
    sq = o * o
    ss0 = jnp.sum(jnp.where(first, sq, 0.0), axis=1, keepdims=True)
    ss1 = jnp.sum(jnp.where(first, 0.0, sq), axis=1, keepdims=True)
    ms = jnp.where(first, ss0, ss1) * (1.0 / HEAD_DIM)
    o_ref[...] = (o * lax.rsqrt(ms + EPS) * g_ref[...]).astype(o_ref.dtype)


def _diff_prompt_body(q_ref, k_ref, v_ref, km_ref, vm_ref, lam_ref, g_ref, o_ref, *, lam_init):
    tq = q_ref.shape[0]
    i = pl.program_id(2)
    qs = _stack_heads(q_ref[...])
    n2 = 2 * tq
    state = (jnp.full((n2, 1), NEG_BIG, F32), jnp.zeros((n2, 1), F32), jnp.zeros((n2, LANES), F32))
    mcol = lax.broadcasted_iota(jnp.int32, (n2, km_ref.shape[0]), 1)
    state = _softmax_step(qs, km_ref[...], vm_ref[...], mcol < N_META, state)

    def body(j, st):
        off = pl.multiple_of(j * tq, tq)
        return _softmax_step(qs, k_ref[pl.ds(off, tq), :], v_ref[pl.ds(off, tq), :], None, st)

    state = lax.fori_loop(0, i, body, state)
    off = pl.multiple_of(i * tq, tq)
    row = lax.broadcasted_iota(jnp.int32, (n2, tq), 0)
    row = jnp.where(row >= tq, row - tq, row)
    col = lax.broadcasted_iota(jnp.int32, (n2, tq), 1)
    mask = (col // CHUNK) <= (row // CHUNK)
    state = _softmax_step(qs, k_ref[pl.ds(off, tq), :], v_ref[pl.ds(off, tq), :], mask, state)
    _diff_finish(state, lam_ref, g_ref, o_ref, lam_init, tq)


def _stick_prompt_body(q_ref, k_ref, v_ref, km_ref, vm_ref, g_ref, o_ref):
    tq = q_ref.shape[0]
    i = pl.program_id(2)
    qs = _stack_heads(q_ref[...])
    n2 = 2 * tq
    tri = _tri(tq)
    state = (jnp.zeros((n2, 1), F32), jnp.zeros((n2, LANES), F32))
    off = pl.multiple_of(i * tq, tq)
    row = lax.broadcasted_iota(jnp.int32, (n2, tq), 0)
    row = jnp.where(row >= tq, row - tq, row)
    col = lax.broadcasted_iota(jnp.int32, (n2, tq), 1)
    state = _stick_step(qs, k_ref[pl.ds(off, tq), :], v_ref[pl.ds(off, tq), :], tri, col < row, state)

    def body(n, st):
        o = pl.multiple_of((i - 1 - n) * tq, tq)
        return _stick_step(qs, k_ref[pl.ds(o, tq), :], v_ref[pl.ds(o, tq), :], tri, None, st)

    state = lax.fori_loop(0, i, body, state)
    nm = km_ref.shape[0]
    mcol = lax.broadcasted_iota(jnp.int32, (n2, nm), 1)
    state = _stick_step(qs, km_ref[...], vm_ref[...], tri[:nm, :nm], mcol < N_META, state)
    _stick_finish(state, g_ref, o_ref, tq)


def _prompt_attention(qkv_bf, meta_bf, batch, seq, diff_lambda, a_g, b_g, lam_init):
    tq = ATT_BLOCK
    nq = seq // tq
    n = batch * seq
    qmap = lambda off: (lambda b, h, i: (b * nq + i, off + h))
    kvmap = lambda off: (lambda b, h, i: (b, off + h))
    mmap = lambda off: (lambda b, h, i: (0, off + h))
    full = lambda b, h, i: (0, 0)
    nm = meta_bf.shape[0]

    def specs(cq, ck, cv):
        return [pl.BlockSpec((tq, LANES), qmap(cq)),
                pl.BlockSpec((seq, LANES), kvmap(ck)), pl.BlockSpec((seq, LANES), kvmap(cv)),
                pl.BlockSpec((nm, LANES), mmap(ck)), pl.BlockSpec((nm, LANES), mmap(cv))]

    out_spec = pl.BlockSpec((tq, LANES), lambda b, h, i: (b * nq + i, h))
    out_shape = jax.ShapeDtypeStruct((n, A_WIDTH), BF16)
    sem = ("parallel", "parallel", "arbitrary")
    a_out = pl.pallas_call(
        functools.partial(_diff_prompt_body, lam_init=lam_init),
        grid=(batch, A_HEADS, nq),
        in_specs=specs(COL_AQ, COL_AK, COL_AV) + [pl.BlockSpec((4, HEAD_DIM), full), pl.BlockSpec((1, LANES), full)],
        out_specs=out_spec, out_shape=out_shape, compiler_params=_params(sem), name="diff_prompt",
    )(qkv_bf, qkv_bf, qkv_bf, meta_bf, meta_bf, diff_lambda, a_g.reshape(1, LANES))
    b_g2 = jnp.concatenate([b_g, b_g]).reshape(1, LANES)
    b_out = pl.pallas_call(
        _stick_prompt_body,
        grid=(batch, B_HEADS // 2, nq),
        in_specs=specs(COL_BQ, COL_BK, COL_BV) + [pl.BlockSpec((1, LANES), full)],
        out_specs=out_spec, out_shape=out_shape, compiler_params=_params(sem), name="stick_prompt",
    )(qkv_bf, qkv_bf, qkv_bf, meta_bf, meta_bf, b_g2)
    return a_out, b_out


def _last_block(tail_ref, new_ref):
    tail = tail_ref[...].astype(BF16)
    new = new_ref[...]
    pad = jnp.zeros((LANES - tail.shape[0] - new.shape[0], LANES), BF16)
    return jnp.concatenate([tail, new, pad], axis=0)


def _diff_sample_body(q_ref, kn_ref, vn_ref, kc_ref, vc_ref, kt_ref, vt_ref, lam_ref, g_ref, o_ref, *, lam_init):
    ts = q_ref.shape[0]
    qs = _stack_heads(q_ref[...])
    n2 = 2 * ts
    state = (jnp.full((n2, 1), NEG_BIG, F32), jnp.zeros((n2, 1), F32), jnp.zeros((n2, LANES), F32))
    col = lax.broadcasted_iota(jnp.int32, (n2, LANES), 1)
    state = _softmax_step(qs, _last_block(kt_ref, kn_ref), _last_block(vt_ref, vn_ref),
                          col < kt_ref.shape[0] + ts, state)
    blk = ATT_BLOCK

    def body(j, st):
        off = pl.multiple_of(j * blk, blk)
        return _softmax_step(qs, kc_ref[pl.ds(off, blk), :].astype(BF16), vc_ref[pl.ds(off, blk), :].astype(BF16),
                             None, st)

    state = lax.fori_loop(0, kc_ref.shape[0] // blk, body, state)
    _diff_finish(state, lam_ref, g_ref, o_ref, lam_init, ts)


def _stick_sample_body(q_ref, kn_ref, vn_ref, kc_ref, vc_ref, kt_ref, vt_ref, g_ref, o_ref):
    ts = q_ref.shape[0]
    nt = kt_ref.shape[0]
    qs = _stack_heads(q_ref[...])
    n2 = 2 * ts
    blk = ATT_BLOCK
    tri = _tri(blk)
    state = (jnp.zeros((n2, 1), F32), jnp.zeros((n2, LANES), F32))
    row = lax.broadcasted_iota(jnp.int32, (n2, LANES), 0)
    row = jnp.where(row >= ts, row - ts, row)
    col = lax.broadcasted_iota(jnp.int32, (n2, LANES), 1)
    mask = (col < nt) | ((col < nt + ts) & (col - nt < row))
    state = _stick_step(qs, _last_block(kt_ref, kn_ref), _last_block(vt_ref, vn_ref), tri[:LANES, :LANES], mask, state)
    nb = kc_ref.shape[0] // blk

    def body(n, st):
        off = pl.multiple_of((nb - 1 - n) * blk, blk)
        return _stick_step(qs, kc_ref[pl.ds(off, blk), :].astype(BF16), vc_ref[pl.ds(off, blk), :].astype(BF16),
                           tri, None, st)

    state = lax.fori_loop(0, nb, body, state)
    _stick_finish(state, g_ref, o_ref, ts)


def _sample_attention(qkv_bf, ca_k, ca_v, cb_k, cb_v, dec_batch, dec_seq, diff_lambda, a_g, b_g, lam_init):
    n_past = ca_k.shape[1]
    main = n_past - N_META
    tail_blk = main // N_META
    full = lambda b, h: (0, 0)

    def specs(cq, ck, cv):
        return [pl.BlockSpec((dec_seq, LANES), lambda b, h: (b, cq + h)),
                pl.BlockSpec((dec_seq, LANES), lambda b, h: (b, ck + h)),
                pl.BlockSpec((dec_seq, LANES), lambda b, h: (b, cv + h)),
                pl.BlockSpec((None, main, LANES), lambda b, h: (b, 0, h)),
                pl.BlockSpec((None, main, LANES), lambda b, h: (b, 0, h)),
                pl.BlockSpec((None, N_META, LANES), lambda b, h: (b, tail_blk, h)),
                pl.BlockSpec((None, N_META, LANES), lambda b, h: (b, tail_blk, h))]

    out_spec = pl.BlockSpec((dec_seq, LANES), lambda b, h: (b, h))
    out_shape = jax.ShapeDtypeStruct((dec_batch * dec_seq, A_WIDTH), BF16)
    sem = ("parallel", "parallel")
    a_out = pl.pallas_call(
        functools.partial(_diff_sample_body, lam_init=lam_init),
        grid=(dec_batch, A_HEADS),
        in_specs=specs(COL_AQ, COL_AK, COL_AV) + [pl.BlockSpec((4, HEAD_DIM), full), pl.BlockSpec((1, LANES), full)],
        out_specs=out_spec, out_shape=out_shape, compiler_params=_params(sem), name="diff_sample",
    )(qkv_bf, qkv_bf, qkv_bf, ca_k, ca_v, ca_k, ca_v, diff_lambda, a_g.reshape(1, LANES))
    b_g2 = jnp.concatenate([b_g, b_g]).reshape(1, LANES)
    b_out = pl.pallas_call(
        _stick_sample_body,
        grid=(dec_batch, B_HEADS // 2),
        in_specs=specs(COL_BQ, COL_BK, COL_BV) + [pl.BlockSpec((1, LANES), full)],
        out_specs=out_spec, out_shape=out_shape, compiler_params=_params(sem), name="stick_sample",
    )(qkv_bf, qkv_bf, qkv_bf, cb_k, cb_v, cb_k, cb_v, b_g2)
    return a_out, b_out


def _post_body(x_ref, a_ref, b_ref, wo_ref, g_ref, wq_ref, sk_ref, h_ref, hn_ref, s_ref):
    h = (x_ref[...] + _dot(a_ref[...], wo_ref[:A_WIDTH, :]) + _dot(b_ref[...], wo_ref[A_WIDTH:, :]))
    h_ref[...] = h
    hn = (h * lax.rsqrt(jnp.mean(h * h, axis=-1, keepdims=True) + EPS) * g_ref[...]).astype(BF16)
    hn_ref[...] = hn
    half = D_KEY // 2
    for hp in range(2 * PEER_HEADS):
        q = _dot(hn, wq_ref[:, hp * half:(hp + 1) * half]).astype(BF16)
        s_ref[hp] = _dot_nt(sk_ref[hp % 2], q)


def _post_attention(x2d, a_out, b_out, wo_bf, g_ffn, wq_bf, sk_bf, tm):
    n = x2d.shape[0]
    row = lambda i: (i, 0)
    full = lambda i: (0, 0)
    return pl.pallas_call(
        _post_body,
        grid=(n // tm,),
        in_specs=[pl.BlockSpec((tm, D_MODEL), row), pl.BlockSpec((tm, A_WIDTH), row), pl.BlockSpec((tm, B_WIDTH), row),
                  pl.BlockSpec((MIX_WIDTH, D_MODEL), full), pl.BlockSpec((1, D_MODEL), full),
                  pl.BlockSpec((D_MODEL, PEER_HEADS * D_KEY), full),
                  pl.BlockSpec((2, N_KEYS, D_KEY // 2), lambda i: (0, 0, 0))],
        out_specs=[pl.BlockSpec((tm, D_MODEL), row), pl.BlockSpec((tm, D_MODEL), row),
                   pl.BlockSpec((2 * PEER_HEADS, N_KEYS, tm), lambda i: (0, 0, i))],
        out_shape=[jax.ShapeDtypeStruct((n, D_MODEL), F32), jax.ShapeDtypeStruct((n, D_MODEL), BF16),
                   jax.ShapeDtypeStruct((2 * PEER_HEADS, N_KEYS, n), F32)],
        compiler_params=_params(("parallel",)),
        name="post",
    )(x2d, a_out, b_out, wo_bf, g_ffn.reshape(1, D_MODEL), wq_bf, sk_bf)


def _extract_top(s, iota, take, val_ref, aux, aux_ref):
    n_rows = float(s.shape[0])
    for r in range(take):
        m = jnp.max(s, axis=0, keepdims=True)
        pos = jnp.min(jnp.where(s == m, iota, n_rows), axis=0, keepdims=True)
        sel = iota == pos
        val_ref[r:r + 1, :] = m
        if aux is None:
            aux_ref[r:r + 1, :] = pos
        else:
            aux_ref[r:r + 1, :] = jnp.sum(jnp.where(sel, aux, 0.0), axis=0, keepdims=True)
        s = jnp.where(sel, -jnp.inf, s)


def _topk_body(s_ref, row_ref, col_ref, gate_ref, v0_ref, i0_ref, v1_ref, i1_ref, bs_ref, id_ref,
               rows_sc, cols_sc, gates_sc):
    tt = s_ref.shape[2]
    k = PEER_TOPK
    key_iota = lax.broadcasted_iota(jnp.int32, (N_KEYS, tt), 0).astype(F32)
    n_cand = k + 8 * 7 + 8
    cand_iota = lax.broadcasted_iota(jnp.int32, (n_cand, tt), 0).astype(F32)

    def head(h, carry):
        _extract_top(s_ref[2 * h], key_iota, k, v0_ref, None, i0_ref)
        _extract_top(s_ref[2 * h + 1], key_iota, k, v1_ref, None, i1_ref)
        v1_all, i1_all = v1_ref[...], i1_ref[...]
        cs = [v0_ref[0:1, :] + v1_all]
        ci = [i0_ref[0:1, :] * N_KEYS + i1_all]
        for r in range(1, 8):
            cs.append(v0_ref[r:r + 1, :] + v1_all[:8])
            ci.append(i0_ref[r:r + 1, :] * N_KEYS + i1_all[:8])
        cs.append(v0_ref[8:16, :] + v1_all[0:1])
        ci.append(i0_ref[8:16, :] * N_KEYS + i1_all[0:1])
        _extract_top(jnp.concatenate(cs, axis=0), cand_iota, k, bs_ref, jnp.concatenate(ci, axis=0), id_ref)
        best = bs_ref[...]
        e = jnp.exp(best - best[0:1])
        ids = id_ref[...]
        rows = jnp.floor(ids * (1.0 / N_KEYS))
        dst = pl.ds(pl.multiple_of(h * k, k), k)
        gates_sc[dst, :] = e / jnp.sum(e, axis=0, keepdims=True)
        rows_sc[dst, :] = rows
        cols_sc[dst, :] = ids - rows * N_KEYS
        return carry

    lax.fori_loop(0, PEER_HEADS, head, 0)
    row_ref[...] = rows_sc[...].T
    col_ref[...] = cols_sc[...].T
    gate_ref[...] = gates_sc[...].T


def _peer_topk(scores):
    n = scores.shape[2]
    tt = TOPK_TOKENS
    slots = PEER_HEADS * PEER_TOPK
    out = jax.ShapeDtypeStruct((n, slots), F32)
    small = lambda dt: pltpu.VMEM((PEER_TOPK, tt), dt)
    return pl.pallas_call(
        _topk_body,
        grid=(n // tt,),
        in_specs=[pl.BlockSpec((2 * PEER_HEADS, N_KEYS, tt), lambda i: (0, 0, i))],
        out_specs=[pl.BlockSpec((tt, slots), lambda i: (i, 0))] * 3,
        out_shape=[out] * 3,
        scratch_shapes=[small(F32)] * 6 + [pltpu.VMEM((slots, tt), F32)] * 3,
        compiler_params=_params(("parallel",)),
        name="topk",
    )(scores)


def _gelu(x):
    return 0.5 * x * (1.0 + lax.erf(x * (2.0 ** -0.5)))


def _peer_body(hn_ref, row_ref, col_ref, gate_ref, u_ref, v_ref, h_ref, gf_ref, y_ref, w_sc, acc_sc):
    tb = hn_ref.shape[0]
    c = pl.program_id(1)
    per_chunk = u_ref.shape[0] // N_KEYS

    @pl.when(c == 0)
    def _():
        acc_sc[...] = jnp.zeros_like(acc_sc)
        sub = lax.broadcasted_iota(jnp.int32, (N_KEYS, LANES), 0).astype(F32)

        def build(t, carry):
            r = row_ref[pl.ds(t, 1), :]
            cc = col_ref[pl.ds(t, 1), :]
            g = gate_ref[pl.ds(t, 1), :]
            at = jnp.where(sub == r, g, 0.0).astype(BF16)
            bt = jnp.where(sub == cc, 1.0, 0.0).astype(BF16)
            w_sc[pl.ds(pl.multiple_of(t * N_KEYS, N_KEYS), N_KEYS), :] = _dot_nt(at, bt)
            return carry

        lax.fori_loop(0, tb, build, 0)

    x = hn_ref[...]
    pre = _dot_nt(x, u_ref[...])
    w = jnp.concatenate([w_sc[pl.ds(c * per_chunk + a, tb, stride=N_KEYS), :] for a in range(per_chunk)], axis=1)
    act = (_gelu(pre) * w).astype(BF16)
    acc_sc[...] += _dot(act, v_ref[...])

    @pl.when(c == pl.num_programs(1) - 1)
    def _():
        h = h_ref[...] + acc_sc[...]
        y_ref[...] = h * lax.rsqrt(jnp.mean(h * h, axis=-1, keepdims=True) + EPS) * gf_ref[...]


def _peer_dense(hn, rows, cols, gates, u_bf, v_bf, h, g_final):
    n = hn.shape[0]
    tb = PEER_TOKENS
    slots = PEER_HEADS * PEER_TOPK
    tok = lambda i, c: (i, 0)
    chunk = lambda i, c: (c, 0)
    return pl.pallas_call(
        _peer_body,
        grid=(n // tb, N_EXPERTS // PEER_CHUNK),
        in_specs=[pl.BlockSpec((tb, D_MODEL), tok), pl.BlockSpec((tb, slots), tok), pl.BlockSpec((tb, slots), tok),
                  pl.BlockSpec((tb, slots), tok), pl.BlockSpec((PEER_CHUNK, D_MODEL), chunk),
                  pl.BlockSpec((PEER_CHUNK, D_MODEL), chunk), pl.BlockSpec((tb, D_MODEL), tok),
                  pl.BlockSpec((1, D_MODEL), lambda i, c: (0, 0))],
        out_specs=pl.BlockSpec((tb, D_MODEL), tok),
        out_shape=jax.ShapeDtypeStruct((n, D_MODEL), F32),
        scratch_shapes=[pltpu.VMEM((tb * N_KEYS, LANES), F32), pltpu.VMEM((tb, D_MODEL), F32)],
        compiler_params=_params(("parallel", "arbitrary")),
        name="peer",
    )(hn, rows, cols, gates, u_bf, v_bf, h, g_final.reshape(1, D_MODEL))


def _finish(x2d, a_out, b_out, wo_bf, g_ffn, wq_bf, sk_bf, u_bf, v_bf, g_final):
    h, hn, scores = _post_attention(x2d, a_out, b_out, wo_bf, g_ffn, wq_bf, sk_bf, ROW_TILE)
    rows, cols, gates = _peer_topk(scores)
    return _peer_dense(hn, rows, cols, gates, u_bf, v_bf, h, g_final)


def kernel(x_prompt, x_sample, cache_a_k, cache_a_v, cache_b_k, cache_b_v, meta_tokens, g_attn, w_qkv,
           diff_lambda, a_norm_g, b_norm_g, w_o, g_ffn, w_peer_q, peer_sub_keys, peer_u, peer_v, g_final):
    batch, seq, _ = x_prompt.shape
    dec_batch, dec_seq, _ = x_sample.shape
    depth = w_qkv.shape[0]
    assert depth == 1, "single-layer step"
    n_past = cache_a_k.shape[2]
    lam_init = 0.8 - 0.6 * math.exp(-0.3 * 0)

    w_bf = w_qkv[0].astype(BF16)
    wo_bf = w_o[0].astype(BF16)
    wq_bf = w_peer_q[0].astype(BF16)
    sk_bf = peer_sub_keys[0].astype(BF16)
    u_bf = peer_u[0].astype(BF16)
    v_bf = peer_v[0].astype(BF16)
    g1, g2 = g_attn[0], g_ffn[0]
    lam_p, a_g, b_g = diff_lambda[0], a_norm_g[0], b_norm_g[0]

    xp = x_prompt.reshape(batch * seq, D_MODEL)
    xs = x_sample.reshape(dec_batch * dec_seq, D_MODEL)
    p_ak, p_av, p_bk, p_bv, p_bf = _qkv_project(xp, N_META + jnp.arange(seq, dtype=jnp.int32), g1, w_bf, ROW_TILE)
    m_ak, m_av, m_bk, m_bv, m_bf = _qkv_project(meta_tokens.astype(F32), jnp.arange(N_META, dtype=jnp.int32),
                                                g1, w_bf, N_META)
    s_ak, s_av, s_bk, s_bv, s_bf = _qkv_project(xs, n_past + jnp.arange(dec_seq, dtype=jnp.int32), g1, w_bf, dec_seq)

    meta_pad = jnp.pad(m_bf, ((0, LANES - N_META), (0, 0)))
    pa, pb = _prompt_attention(p_bf, meta_pad, batch, seq, lam_p, a_g, b_g, lam_init)
    y_prompt = _finish(xp, pa, pb, wo_bf, g2, wq_bf, sk_bf, u_bf, v_bf, g_final)

    ca_k = cache_a_k[0].reshape(dec_batch, n_past, A_WIDTH)
    ca_v = cache_a_v[0].reshape(dec_batch, n_past, A_WIDTH)
    cb_k = cache_b_k[0].reshape(dec_batch, n_past, B_WIDTH)
    cb_v = cache_b_v[0].reshape(dec_batch, n_past, B_WIDTH)
    sa, sb = _sample_attention(s_bf, ca_k, ca_v, cb_k, cb_v, dec_batch, dec_seq, lam_p, a_g, b_g, lam_init)
    y_sample = _finish(xs, sa, sb, wo_bf, g2, wq_bf, sk_bf, u_bf, v_bf, g_final)

    def prompt_cache(meta_rows, frame_rows, heads):
        m = jnp.broadcast_to(meta_rows[None], (batch, N_META, A_WIDTH))
        full = jnp.concatenate([m, frame_rows.reshape(batch, seq, A_WIDTH)], axis=1)
        return full.reshape(1, batch, N_META + seq, heads, A_WIDTH // heads)

    def sample_cache(rows, heads):
        return rows.reshape(1, dec_batch, dec_seq, heads, A_WIDTH // heads)

    return (y_prompt.reshape(batch, seq, D_MODEL), y_sample.reshape(dec_batch, dec_seq, D_MODEL),
            prompt_cache(m_ak, p_ak, A_HEADS), prompt_cache(m_av, p_av, A_HEADS),
            prompt_cache(m_bk, p_bk, B_HEADS), prompt_cache(m_bv, p_bv, B_HEADS),
            sample_cache(s_ak, A_HEADS), sample_cache(s_av, A_HEADS),
            sample_cache(s_bk, B_HEADS), sample_cache(s_bv, B_HEADS))
```

```python
import functools
import math

import jax
import jax.numpy as jnp
from jax import lax
from jax.experimental import pallas as pl
from jax.experimental.pallas import tpu as pltpu

F32 = jnp.float32
BF16 = jnp.bfloat16

D_MODEL = 1024
CHUNK = 64
N_META = 16
HEAD_DIM = 64
A_HEADS = 4
B_HEADS = 8
A_WIDTH = A_HEADS * 2 * HEAD_DIM
B_WIDTH = B_HEADS * HEAD_DIM
MIX_WIDTH = A_WIDTH + B_WIDTH
QKV_WIDTH = 3 * MIX_WIDTH
ROT_DIM = HEAD_DIM // 4
ROPE_THETA = 500000.0
N_KEYS = 128
N_EXPERTS = N_KEYS * N_KEYS
PEER_HEADS = 8
PEER_TOPK = 16
D_KEY = 256
EPS = 1e-6
NEG_BIG = -1e30

LANES = 128
ROW_TILE = 256
ATT_BLOCK = 256
TOPK_TOKENS = 128
PEER_TOKENS = 256
PEER_CHUNK = 2048
PEER_SUB = 512
PEER_BUILD_UNROLL = 8
ATT_ROWS = 512
STICK_EXIT = -110.0
VMEM_LIMIT = 56 * 1024 * 1024

COL_AQ, COL_AK, COL_AV = 0, 4, 8
COL_BQ, COL_BK, COL_BV = 12, 16, 20


def _dot(a, b):
    return jnp.dot(a, b, preferred_element_type=F32)


def _dot_nt(a, b):
    return lax.dot_general(a, b, (((1,), (1,)), ((), ())), preferred_element_type=F32)


def _params(sem):
    return pltpu.CompilerParams(dimension_semantics=sem, vmem_limit_bytes=VMEM_LIMIT)


def _qkv_body(x_ref, g_ref, w_ref, c_ref, s1_ref, s2_ref, ak_ref, av_ref, bk_ref, bv_ref, bf_ref):
    x = x_ref[...]
    xn = (x * lax.rsqrt(jnp.mean(x * x, axis=-1, keepdims=True) + EPS) * g_ref[...]).astype(BF16)
    cos_t = c_ref[...]
    sin_lo = s1_ref[...]
    sin_hi = s2_ref[...]
    f32_outs = {1: ak_ref, 2: av_ref, 4: bk_ref, 5: bv_ref}
    for grp in range(6):
        for j in range(4):
            cb = grp * 4 + j
            y = _dot(xn, w_ref[:, cb * LANES:(cb + 1) * LANES])
            if grp in (0, 1):
                y = (y * cos_t + pltpu.roll(y, LANES - ROT_DIM // 2, 1) * sin_lo
                     + pltpu.roll(y, ROT_DIM // 2, 1) * sin_hi)
            if grp in f32_outs:
                f32_outs[grp][:, j * LANES:(j + 1) * LANES] = y
            if grp in (0, 3):
                y = y * (HEAD_DIM ** -0.5)
            bf_ref[:, cb * LANES:(cb + 1) * LANES] = y.astype(BF16)


def _rope_tables(pos):
    half = ROT_DIM // 2
    inv_freq = ROPE_THETA ** (-jnp.arange(half, dtype=F32) * 2.0 / ROT_DIM)
    ang = pos.astype(F32)[:, None] * inv_freq[None, :]
    cos, sin = jnp.cos(ang), jnp.sin(ang)
    t = pos.shape[0]
    pad = jnp.zeros((t, HEAD_DIM - ROT_DIM), F32)
    zero = jnp.zeros((t, half), F32)
    cos_t = jnp.concatenate([cos, cos, pad + 1.0], axis=1)
    sin_lo = jnp.concatenate([-sin, zero, pad], axis=1)
    sin_hi = jnp.concatenate([zero, sin, pad], axis=1)
    tile = lambda a: jnp.concatenate([a, a], axis=1)
    return tile(cos_t), tile(sin_lo), tile(sin_hi)


def _qkv_project(x2d, pos, g, w_bf, tm):
    n = x2d.shape[0]
    t = pos.shape[0]
    per = t // tm
    tabs = _rope_tables(pos)
    row = lambda i: (i, 0)
    tab = lambda i: (i % per, 0)
    full = lambda i: (0, 0)
    f32_out = jax.ShapeDtypeStruct((n, A_WIDTH), F32)
    return pl.pallas_call(
        _qkv_body,
        grid=(n // tm,),
        in_specs=[pl.BlockSpec((tm, D_MODEL), row), pl.BlockSpec((1, D_MODEL), full),
                  pl.BlockSpec((D_MODEL, QKV_WIDTH), full),
                  pl.BlockSpec((tm, LANES), tab), pl.BlockSpec((tm, LANES), tab), pl.BlockSpec((tm, LANES), tab)],
        out_specs=[pl.BlockSpec((tm, A_WIDTH), row)] * 4 + [pl.BlockSpec((tm, QKV_WIDTH), row)],
        out_shape=[f32_out] * 4 + [jax.ShapeDtypeStruct((n, QKV_WIDTH), BF16)],
        compiler_params=_params(("parallel",)),
        name="qkv",
    )(x2d, g.reshape(1, D_MODEL), w_bf, *tabs)


def _stack_heads(q):
    lane = lax.broadcasted_iota(jnp.int32, q.shape, 1)
    zero = jnp.zeros_like(q)
    return jnp.concatenate([jnp.where(lane < HEAD_DIM, q, zero), jnp.where(lane >= HEAD_DIM, q, zero)], axis=0)


def _row_tiles(n2):
    sub = min(ATT_ROWS, n2)
    return [slice(r, r + sub) for r in range(0, n2, sub)]


def _tile_iotas(rows, n_keys, period):
    shape = (rows.stop - rows.start, n_keys)
    row = (lax.broadcasted_iota(jnp.int32, shape, 0) + rows.start) % period
    col = lax.broadcasted_iota(jnp.int32, shape, 1)
    return row, col


def _softmax_init(n2):
    return jnp.full((n2, 1), NEG_BIG, F32), jnp.zeros((n2, 1), F32), jnp.zeros((n2, LANES), F32)


def _softmax_step(qs, kb, vb, mask, state):
    m, l, acc = state
    s = _dot_nt(qs, kb)
    if mask is not None:
        s = jnp.where(mask, s, NEG_BIG)
    m_new = jnp.maximum(m, jnp.max(s, axis=1, keepdims=True))
    alpha = jnp.exp(m - m_new)
    p = jnp.exp(s - m_new)
    l = alpha * l + jnp.sum(p, axis=1, keepdims=True)
    acc = alpha * acc + _dot(p.astype(BF16), vb)
    return m_new, l, acc


def _diff_finish(state, lam_ref, g_ref, o_ref, lam_init):
    tq = o_ref.shape[0]
    _, l, acc = state
    lp = lam_ref[...]
    lam = (jnp.exp(jnp.sum(lp[0:1] * lp[1:2], axis=1, keepdims=True))
           - jnp.exp(jnp.sum(lp[2:3] * lp[3:4], axis=1, keepdims=True)) + lam_init)
    o = acc[:tq] / l[:tq] - lam * (acc[tq:] / l[tq:])
    o = o * lax.rsqrt(jnp.mean(o * o, axis=1, keepdims=True) + EPS) * g_ref[...] * (1.0 - lam_init)
    o_ref[...] = o.astype(o_ref.dtype)


def _stick_init(c_sc, acc_sc):
    c_sc[...] = jnp.zeros(c_sc.shape, F32)
    acc_sc[...] = jnp.zeros(acc_sc.shape, F32)


def _stick_step(qs_sc, kb, vb, tri, mask_fn, c_sc, acc_sc):
    for rows in _row_tiles(qs_sc.shape[0]):
        z = _dot_nt(qs_sc[rows, :], kb)
        lm = -(jnp.maximum(z, 0.0) + jnp.log1p(jnp.exp(-jnp.abs(z))))
        mask = None if mask_fn is None else mask_fn(rows)
        if mask is not None:
            lm = jnp.where(mask, lm, 0.0)
        hi = lm.astype(BF16)
        lo = (lm - hi.astype(F32)).astype(BF16)
        cum = _dot(hi, tri) + _dot(lo, tri)
        c = c_sc[rows, :]
        a = jnp.exp(z + cum + c)
        if mask is not None:
            a = jnp.where(mask, a, 0.0)
        acc_sc[rows, :] += _dot(a.astype(BF16), vb)
        c_sc[rows, :] = c + cum[:, 0:1]


def _tri(n):
    r = lax.broadcasted_iota(jnp.int32, (n, n), 0)
    c = lax.broadcasted_iota(jnp.int32, (n, n), 1)
    return jnp.where(r >= c, 1.0, 0.0).astype(BF16)


def _stick_finish(acc_sc, g_ref, o_ref):
    tq = o_ref.shape[0]
    lane = lax.broadcasted_iota(jnp.int32, (tq, LANES), 1)
    first = lane < HEAD_DIM
    o = jnp.where(first, acc_sc[:tq, :], acc_sc[tq:, :])
    sq = o * o
    ss0 = jnp.sum(jnp.where(first, sq, 0.0), axis=1, keepdims=True)
    ss1 = jnp.sum(jnp.where(first, 0.0, sq), axis=1, keepdims=True)
    ms = jnp.where(first, ss0, ss1) * (1.0 / HEAD_DIM)
    o_ref[...] = (o * lax.rsqrt(ms + EPS) * g_ref[...]).astype(o_ref.dtype)


def _stick_scan(n_blocks, step_fn, c_sc):
    def cond(st):
        n, cmax = st
        return jnp.logical_and(n < n_blocks, cmax > STICK_EXIT)

    def body(st):
        step_fn(st[0])
        return st[0] + 1, jnp.max(c_sc[...])

    return lax.while_loop(cond, body, (jnp.int32(0), jnp.max(c_sc[...])))[1]


def _stick_scratch(n2):
    return [pltpu.VMEM((n2, LANES), BF16), pltpu.VMEM((n2, 1), F32), pltpu.VMEM((n2, LANES), F32)]


def _diff_prompt_body(q_ref, k_ref, v_ref, km_ref, vm_ref, lam_ref, g_ref, o_ref, *, lam_init):
    tq = q_ref.shape[0]
    i = pl.program_id(2)
    qs = _stack_heads(q_ref[...])
    rows = slice(0, 2 * tq)
    nm = km_ref.shape[0]
    state = _softmax_step(qs, km_ref[...], vm_ref[...], _tile_iotas(rows, nm, tq)[1] < N_META, _softmax_init(2 * tq))

    def body(j, st):
        off = pl.multiple_of(j * tq, tq)
        return _softmax_step(qs, k_ref[pl.ds(off, tq), :], v_ref[pl.ds(off, tq), :], None, st)

    state = lax.fori_loop(0, i, body, state)
    row, col = _tile_iotas(rows, tq, tq)
    off = pl.multiple_of(i * tq, tq)
    state = _softmax_step(qs, k_ref[pl.ds(off, tq), :], v_ref[pl.ds(off, tq), :],
                          (col // CHUNK) <= (row // CHUNK), state)
    _diff_finish(state, lam_ref, g_ref, o_ref, lam_init)


def _stick_prompt_body(q_ref, k_ref, v_ref, km_ref, vm_ref, g_ref, o_ref, qs_sc, c_sc, acc_sc):
    tq = q_ref.shape[0]
    i = pl.program_id(2)
    qs_sc[...] = _stack_heads(q_ref[...])
    _stick_init(c_sc, acc_sc)
    tri = _tri(tq)

    def causal_mask(rows):
        row, col = _tile_iotas(rows, tq, tq)
        return col < row

    off = pl.multiple_of(i * tq, tq)
    _stick_step(qs_sc, k_ref[pl.ds(off, tq), :], v_ref[pl.ds(off, tq), :], tri, causal_mask, c_sc, acc_sc)

    def step(n):
        o = pl.multiple_of((i - 1 - n) * tq, tq)
        _stick_step(qs_sc, k_ref[pl.ds(o, tq), :], v_ref[pl.ds(o, tq), :], tri, None, c_sc, acc_sc)

    cmax = _stick_scan(i, step, c_sc)
    nm = km_ref.shape[0]

    @pl.when(cmax > STICK_EXIT)
    def _():
        _stick_step(qs_sc, km_ref[...], vm_ref[...], tri[:nm, :nm],
                    lambda rows: _tile_iotas(rows, nm, tq)[1] < N_META, c_sc, acc_sc)

    _stick_finish(acc_sc, g_ref, o_ref)


def _prompt_attention(qkv_bf, meta_bf, batch, seq, diff_lambda, a_g, b_g, lam_init):
    tq = ATT_BLOCK
    nq = seq // tq
    n = batch * seq
    qmap = lambda off: (lambda b, h, i: (b * nq + i, off + h))
    kvmap = lambda off: (lambda b, h, i: (b, off + h))
    mmap = lambda off: (lambda b, h, i: (0, off + h))
    full = lambda b, h, i: (0, 0)
    nm = meta_bf.shape[0]

    def specs(cq, ck, cv):
        return [pl.BlockSpec((tq, LANES), qmap(cq)),
                pl.BlockSpec((seq, LANES), kvmap(ck)), pl.BlockSpec((seq, LANES), kvmap(cv)),
                pl.BlockSpec((nm, LANES), mmap(ck)), pl.BlockSpec((nm, LANES), mmap(cv))]

    out_spec = pl.BlockSpec((tq, LANES), lambda b, h, i: (b * nq + i, h))
    out_shape = jax.ShapeDtypeStruct((n, A_WIDTH), BF16)
    sem = ("parallel", "parallel", "arbitrary")
    a_out = pl.pallas_call(
        functools.partial(_diff_prompt_body, lam_init=lam_init),
        grid=(batch, A_HEADS, nq),
        in_specs=specs(COL_AQ, COL_AK, COL_AV) + [pl.BlockSpec((4, HEAD_DIM), full), pl.BlockSpec((1, LANES), full)],
        out_specs=out_spec, out_shape=out_shape,
        compiler_params=_params(sem), name="diff_prompt",
    )(qkv_bf, qkv_bf, qkv_bf, meta_bf, meta_bf, diff_lambda, a_g.reshape(1, LANES))
    b_g2 = jnp.concatenate([b_g, b_g]).reshape(1, LANES)
    b_out = pl.pallas_call(
        _stick_prompt_body,
        grid=(batch, B_HEADS // 2, nq),
        in_specs=specs(COL_BQ, COL_BK, COL_BV) + [pl.BlockSpec((1, LANES), full)],
        out_specs=out_spec, out_shape=out_shape, scratch_shapes=_stick_scratch(2 * tq),
        compiler_params=_params(sem), name="stick_prompt",
    )(qkv_bf, qkv_bf, qkv_bf, meta_bf, meta_bf, b_g2)
    return a_out, b_out


def _last_block(tail_ref, new_ref):
    tail = tail_ref[...].astype(BF16)
    new = new_ref[...]
    pad = jnp.zeros((LANES - tail.shape[0] - new.shape[0], LANES), BF16)
    return jnp.concatenate([tail, new, pad], axis=0)


def _diff_sample_body(q_ref, kn_ref, vn_ref, kc_ref, vc_ref, kt_ref, vt_ref, lam_ref, g_ref, o_ref, *, lam_init):
    ts = q_ref.shape[0]
    qs = _stack_heads(q_ref[...])
    n_last = kt_ref.shape[0] + ts
    col = _tile_iotas(slice(0, 2 * ts), LANES, ts)[1]
    state = _softmax_step(qs, _last_block(kt_ref, kn_ref), _last_block(vt_ref, vn_ref), col < n_last,
                          _softmax_init(2 * ts))
    blk = ATT_BLOCK

    def body(j, st):
        off = pl.multiple_of(j * blk, blk)
        return _softmax_step(qs, kc_ref[pl.ds(off, blk), :].astype(BF16), vc_ref[pl.ds(off, blk), :].astype(BF16),
                             None, st)

    state = lax.fori_loop(0, kc_ref.shape[0] // blk, body, state)
    _diff_finish(state, lam_ref, g_ref, o_ref, lam_init)


def _stick_sample_body(q_ref, kn_ref, vn_ref, kc_ref, vc_ref, kt_ref, vt_ref, g_ref, o_ref, qs_sc, c_sc, acc_sc):
    ts = q_ref.shape[0]
    nt = kt_ref.shape[0]
    qs_sc[...] = _stack_heads(q_ref[...])
    _stick_init(c_sc, acc_sc)
    blk = ATT_BLOCK
    tri = _tri(blk)

    def last_mask(rows):
        row, col = _tile_iotas(rows, LANES, ts)
        return (col < nt) | ((col < nt + ts) & (col - nt < row))

    _stick_step(qs_sc, _last_block(kt_ref, kn_ref), _last_block(vt_ref, vn_ref), tri[:LANES, :LANES], last_mask,
                c_sc, acc_sc)
    nb = kc_ref.shape[0] // blk

    def step(n):
        off = pl.multiple_of((nb - 1 - n) * blk, blk)
        _stick_step(qs_sc, kc_ref[pl.ds(off, blk), :].astype(BF16), vc_ref[pl.ds(off, blk), :].astype(BF16),
                    tri, None, c_sc, acc_sc)

    _stick_scan(nb, step, c_sc)
    _stick_finish(acc_sc, g_ref, o_ref)


def _sample_attention(qkv_bf, ca_k, ca_v, cb_k, cb_v, dec_batch, dec_seq, diff_lambda, a_g, b_g, lam_init):
    n_past = ca_k.shape[1]
    main = n_past - N_META
    tail_blk = main // N_META
    full = lambda b, h: (0, 0)

    def specs(cq, ck, cv):
        return [pl.BlockSpec((dec_seq, LANES), lambda b, h: (b, cq + h)),
                pl.BlockSpec((dec_seq, LANES), lambda b, h: (b, ck + h)),
                pl.BlockSpec((dec_seq, LANES), lambda b, h: (b, cv + h)),
                pl.BlockSpec((None, main, LANES), lambda b, h: (b, 0, h)),
                pl.BlockSpec((None, main, LANES), lambda b, h: (b, 0, h)),
                pl.BlockSpec((None, N_META, LANES), lambda b, h: (b, tail_blk, h)),
                pl.BlockSpec((None, N_META, LANES), lambda b, h: (b, tail_blk, h))]

    out_spec = pl.BlockSpec((dec_seq, LANES), lambda b, h: (b, h))
    out_shape = jax.ShapeDtypeStruct((dec_batch * dec_seq, A_WIDTH), BF16)
    sem = ("parallel", "parallel")
    a_out = pl.pallas_call(
        functools.partial(_diff_sample_body, lam_init=lam_init),
        grid=(dec_batch, A_HEADS),
        in_specs=specs(COL_AQ, COL_AK, COL_AV) + [pl.BlockSpec((4, HEAD_DIM), full), pl.BlockSpec((1, LANES), full)],
        out_specs=out_spec, out_shape=out_shape,
        compiler_params=_params(sem), name="diff_sample",
    )(qkv_bf, qkv_bf, qkv_bf, ca_k, ca_v, ca_k, ca_v, diff_lambda, a_g.reshape(1, LANES))
    b_g2 = jnp.concatenate([b_g, b_g]).reshape(1, LANES)
    b_out = pl.pallas_call(
        _stick_sample_body,
        grid=(dec_batch, B_HEADS // 2),
        in_specs=specs(COL_BQ, COL_BK, COL_BV) + [pl.BlockSpec((1, LANES), full)],
        out_specs=out_spec, out_shape=out_shape, scratch_shapes=_stick_scratch(2 * dec_seq),
        compiler_params=_params(sem), name="stick_sample",
    )(qkv_bf, qkv_bf, qkv_bf, cb_k, cb_v, cb_k, cb_v, b_g2)
    return a_out, b_out


def _post_body(x_ref, a_ref, b_ref, wo_ref, g_ref, wq_ref, sk_ref, h_ref, hn_ref, s_ref):
    h = (x_ref[...] + _dot(a_ref[...], wo_ref[:A_WIDTH, :]) + _dot(b_ref[...], wo_ref[A_WIDTH:, :]))
    h_ref[...] = h
    hn = (h * lax.rsqrt(jnp.mean(h * h, axis=-1, keepdims=True) + EPS) * g_ref[...]).astype(BF16)
    hn_ref[...] = hn
    half = D_KEY // 2
    for hp in range(2 * PEER_HEADS):
        q = _dot(hn, wq_ref[:, hp * half:(hp + 1) * half]).astype(BF16)
        s_ref[hp] = _dot_nt(sk_ref[hp % 2], q)


def _post_attention(x2d, a_out, b_out, wo_bf, g_ffn, wq_bf, sk_bf, tm):
    n = x2d.shape[0]
    row = lambda i: (i, 0)
    full = lambda i: (0, 0)
    return pl.pallas_call(
        _post_body,
        grid=(n // tm,),
        in_specs=[pl.BlockSpec((tm, D_MODEL), row), pl.BlockSpec((tm, A_WIDTH), row), pl.BlockSpec((tm, B_WIDTH), row),
                  pl.BlockSpec((MIX_WIDTH, D_MODEL), full), pl.BlockSpec((1, D_MODEL), full),
                  pl.BlockSpec((D_MODEL, PEER_HEADS * D_KEY), full),
                  pl.BlockSpec((2, N_KEYS, D_KEY // 2), lambda i: (0, 0, 0))],
        out_specs=[pl.BlockSpec((tm, D_MODEL), row), pl.BlockSpec((tm, D_MODEL), row),
                   pl.BlockSpec((2 * PEER_HEADS, N_KEYS, tm), lambda i: (0, 0, i))],
        out_shape=[jax.ShapeDtypeStruct((n, D_MODEL), F32), jax.ShapeDtypeStruct((n, D_MODEL), BF16),
                   jax.ShapeDtypeStruct((2 * PEER_HEADS, N_KEYS, n), F32)],
        compiler_params=_params(("parallel",)),
        name="post",
    )(x2d, a_out, b_out, wo_bf, g_ffn.reshape(1, D_MODEL), wq_bf, sk_bf)


def _extract_top(s, iota, take, val_ref, aux, aux_ref):
    n_rows = float(s.shape[0])
    for r in range(take):
        m = jnp.max(s, axis=0, keepdims=True)
        pos = jnp.min(jnp.where(s == m, iota, n_rows), axis=0, keepdims=True)
        sel = iota == pos
        val_ref[r:r + 1, :] = m
        if aux is None:
            aux_ref[r:r + 1, :] = pos
        else:
            aux_ref[r:r + 1, :] = jnp.sum(jnp.where(sel, aux, 0.0), axis=0, keepdims=True)
        s = jnp.where(sel, -jnp.inf, s)


def _topk_body(s_ref, row_ref, col_ref, gate_ref, v0_ref, i0_ref, v1_ref, i1_ref, bs_ref, id_ref,
               rows_sc, cols_sc, gates_sc):
    tt = s_ref.shape[2]
    k = PEER_TOPK
    key_iota = lax.broadcasted_iota(jnp.int32, (N_KEYS, tt), 0).astype(F32)
    n_cand = k + 8 * 7 + 8
    cand_iota = lax.broadcasted_iota(jnp.int32, (n_cand, tt), 0).astype(F32)

    def head(h, carry):
        _extract_top(s_ref[2 * h], key_iota, k, v0_ref, None, i0_ref)
        _extract_top(s_ref[2 * h + 1], key_iota, k, v1_ref, None, i1_ref)
        v1_all, i1_all = v1_ref[...], i1_ref[...]
        cs = [v0_ref[0:1, :] + v1_all]
        ci = [i0_ref[0:1, :] * N_KEYS + i1_all]
        for r in range(1, 8):
            cs.append(v0_ref[r:r + 1, :] + v1_all[:8])
            ci.append(i0_ref[r:r + 1, :] * N_KEYS + i1_all[:8])
        cs.append(v0_ref[8:16, :] + v1_all[0:1])
        ci.append(i0_ref[8:16, :] * N_KEYS + i1_all[0:1])
        _extract_top(jnp.concatenate(cs, axis=0), cand_iota, k, bs_ref, jnp.concatenate(ci, axis=0), id_ref)
        best = bs_ref[...]
        e = jnp.exp(best - best[0:1])
        ids = id_ref[...]
        rows = jnp.floor(ids * (1.0 / N_KEYS))
        dst = pl.ds(pl.multiple_of(h * k, k), k)
        gates_sc[dst, :] = e / jnp.sum(e, axis=0, keepdims=True)
        rows_sc[dst, :] = rows
        cols_sc[dst, :] = ids - rows * N_KEYS
        return carry

    lax.fori_loop(0, PEER_HEADS, head, 0)
    row_ref[...] = rows_sc[...].T
    col_ref[...] = cols_sc[...].T
    gate_ref[...] = gates_sc[...].T


def _peer_topk(scores):
    n = scores.shape[2]
    tt = TOPK_TOKENS
    slots = PEER_HEADS * PEER_TOPK
    out = jax.ShapeDtypeStruct((n, slots), F32)
    return pl.pallas_call(
        _topk_body,
        grid=(n // tt,),
        in_specs=[pl.BlockSpec((2 * PEER_HEADS, N_KEYS, tt), lambda i: (0, 0, i))],
        out_specs=[pl.BlockSpec((tt, slots), lambda i: (i, 0))] * 3,
        out_shape=[out] * 3,
        scratch_shapes=[pltpu.VMEM((PEER_TOPK, tt), F32)] * 6 + [pltpu.VMEM((slots, tt), F32)] * 3,
        compiler_params=_params(("parallel",)),
        name="topk",
    )(scores)


def _gelu(x):
    return 0.5 * x * (1.0 + lax.erf(x * (2.0 ** -0.5)))


def _peer_body(hn_ref, row_ref, col_ref, gate_ref, u_ref, v_ref, h_ref, gf_ref, y_ref, w_sc, acc_sc):
    tb = hn_ref.shape[0]
    c = pl.program_id(1)
    per_chunk = u_ref.shape[0] // N_KEYS

    @pl.when(c == 0)
    def _():
        acc_sc[...] = jnp.zeros_like(acc_sc)
        sub = lax.broadcasted_iota(jnp.int32, (N_KEYS, LANES), 0).astype(F32)

        def build(t, carry):
            r = row_ref[pl.ds(t, 1), :]
            cc = col_ref[pl.ds(t, 1), :]
            g = gate_ref[pl.ds(t, 1), :]
            at = jnp.where(sub == r, g, 0.0).astype(BF16)
            bt = jnp.where(sub == cc, 1.0, 0.0).astype(BF16)
            w_sc[pl.ds(pl.multiple_of(t * N_KEYS, N_KEYS), N_KEYS), :] = _dot_nt(at, bt)
            return carry

        lax.fori_loop(0, tb, build, 0, unroll=PEER_BUILD_UNROLL)

    x = hn_ref[...]
    total = None
    for s in range(u_ref.shape[0] // PEER_SUB):
        experts = slice(s * PEER_SUB, (s + 1) * PEER_SUB)
        pre = _dot_nt(x, u_ref[experts, :])
        first = c * per_chunk + s * (PEER_SUB // N_KEYS)
        w = jnp.concatenate([w_sc[pl.ds(first + a, tb, stride=N_KEYS), :] for a in range(PEER_SUB // N_KEYS)],
                            axis=1)
        part = _dot((_gelu(pre) * w).astype(BF16), v_ref[experts, :])
        total = part if total is None else total + part
    acc_sc[...] += total

    @pl.when(c == pl.num_programs(1) - 1)
    def _():
        h = h_ref[...] + acc_sc[...]
        y_ref[...] = h * lax.rsqrt(jnp.mean(h * h, axis=-1, keepdims=True) + EPS) * gf_ref[...]


def _peer_dense(hn, rows, cols, gates, u_bf, v_bf, h, g_final):
    n = hn.shape[0]
    tb = PEER_TOKENS
    slots = PEER_HEADS * PEER_TOPK
    tok = lambda i, c: (i, 0)
    chunk = lambda i, c: (c, 0)
    return pl.pallas_call(
        _peer_body,
        grid=(n // tb, N_EXPERTS // PEER_CHUNK),
        in_specs=[pl.BlockSpec((tb, D_MODEL), tok), pl.BlockSpec((tb, slots), tok), pl.BlockSpec((tb, slots), tok),
                  pl.BlockSpec((tb, slots), tok), pl.BlockSpec((PEER_CHUNK, D_MODEL), chunk),
                  pl.BlockSpec((PEER_CHUNK, D_MODEL), chunk), pl.BlockSpec((tb, D_MODEL), tok),
                  pl.BlockSpec((1, D_MODEL), lambda i, c: (0, 0))],
        out_specs=pl.BlockSpec((tb, D_MODEL), tok),
        out_shape=jax.ShapeDtypeStruct((n, D_MODEL), F32),
        scratch_shapes=[pltpu.VMEM((tb * N_KEYS, LANES), F32), pltpu.VMEM((tb, D_MODEL), F32)],
        compiler_params=_params(("parallel", "arbitrary")),
        name="peer",
    )(hn, rows, cols, gates, u_bf, v_bf, h, g_final.reshape(1, D_MODEL))


def _finish(x2d, a_out, b_out, wo_bf, g_ffn, wq_bf, sk_bf, u_bf, v_bf, g_final):
    h, hn, scores = _post_attention(x2d, a_out, b_out, wo_bf, g_ffn, wq_bf, sk_bf, ROW_TILE)
    rows, cols, gates = _peer_topk(scores)
    return _peer_dense(hn, rows, cols, gates, u_bf, v_bf, h, g_final)


def kernel(x_prompt, x_sample, cache_a_k, cache_a_v, cache_b_k, cache_b_v, meta_tokens, g_attn, w_qkv,
           diff_lambda, a_norm_g, b_norm_g, w_o, g_ffn, w_peer_q, peer_sub_keys, peer_u, peer_v, g_final):
    batch, seq, _ = x_prompt.shape
    dec_batch, dec_seq, _ = x_sample.shape
    depth = w_qkv.shape[0]
    assert depth == 1, "single-layer step"
    n_past = cache_a_k.shape[2]
    lam_init = 0.8 - 0.6 * math.exp(-0.3 * 0)

    w_bf = w_qkv[0].astype(BF16)
    wo_bf = w_o[0].astype(BF16)
    wq_bf = w_peer_q[0].astype(BF16)
    sk_bf = peer_sub_keys[0].astype(BF16)
    u_bf = peer_u[0].astype(BF16)
    v_bf = peer_v[0].astype(BF16)
    g1, g2 = g_attn[0], g_ffn[0]
    lam_p, a_g, b_g = diff_lambda[0], a_norm_g[0], b_norm_g[0]

    xp = x_prompt.reshape(batch * seq, D_MODEL)
    xs = x_sample.reshape(dec_batch * dec_seq, D_MODEL)
    p_ak, p_av, p_bk, p_bv, p_bf = _qkv_project(xp, N_META + jnp.arange(seq, dtype=jnp.int32), g1, w_bf, ROW_TILE)
    m_ak, m_av, m_bk, m_bv, m_bf = _qkv_project(meta_tokens.astype(F32), jnp.arange(N_META, dtype=jnp.int32),
                                                g1, w_bf, N_META)
    s_ak, s_av, s_bk, s_bv, s_bf = _qkv_project(xs, n_past + jnp.arange(dec_seq, dtype=jnp.int32), g1, w_bf, dec_seq)

    meta_pad = jnp.pad(m_bf, ((0, LANES - N_META), (0, 0)))
    pa, pb = _prompt_attention(p_bf, meta_pad, batch, seq, lam_p, a_g, b_g, lam_init)
    y_prompt = _finish(xp, pa, pb, wo_bf, g2, wq_bf, sk_bf, u_bf, v_bf, g_final)

    ca_k = cache_a_k[0].reshape(dec_batch, n_past, A_WIDTH)
    ca_v = cache_a_v[0].reshape(dec_batch, n_past, A_WIDTH)
    cb_k = cache_b_k[0].reshape(dec_batch, n_past, B_WIDTH)
    cb_v = cache_b_v[0].reshape(dec_batch, n_past, B_WIDTH)
    sa, sb = _sample_attention(s_bf, ca_k, ca_v, cb_k, cb_v, dec_batch, dec_seq, lam_p, a_g, b_g, lam_init)
    y_sample = _finish(xs, sa, sb, wo_bf, g2, wq_bf, sk_bf, u_bf, v_bf, g_final)

    def prompt_cache(meta_rows, frame_rows, heads):
        m = jnp.broadcast_to(meta_rows[None], (batch, N_META, A_WIDTH))
        full = jnp.concatenate([m, frame_rows.reshape(batch, seq, A_WIDTH)], axis=1)
        return full.reshape(1, batch, N_META + seq, heads, A_WIDTH // heads)

    def sample_cache(rows, heads):
        return rows.reshape(1, dec_batch, dec_seq, heads, A_WIDTH // heads)

    return (y_prompt.reshape(batch, seq, D_MODEL), y_sample.reshape(dec_batch, dec_seq, D_MODEL),
            prompt_cache(m_ak, p_ak, A_HEADS), prompt_cache(m_av, p_av, A_HEADS),
            prompt_cache(m_bk, p_bk, B_HEADS), prompt_cache(m_bv, p_bv, B_HEADS),
            sample_cache(s_ak, A_HEADS), sample_cache(s_av, A_HEADS),
            sample_cache(s_bk, B_HEADS), sample_cache(s_bv, B_HEADS))
```

```python
import functools
import math

import jax
import jax.numpy as jnp
from jax import lax
from jax.experimental import pallas as pl
from jax.experimental.pallas import tpu as pltpu

F32 = jnp.float32
BF16 = jnp.bfloat16

D_MODEL = 1024
CHUNK = 64
N_META = 16
HEAD_DIM = 64
A_HEADS = 4
B_HEADS = 8
A_WIDTH = A_HEADS * 2 * HEAD_DIM
B_WIDTH = B_HEADS * HEAD_DIM
MIX_WIDTH = A_WIDTH + B_WIDTH
QKV_WIDTH = 3 * MIX_WIDTH
ROT_DIM = HEAD_DIM // 4
ROPE_THETA = 500000.0
N_KEYS = 128
N_EXPERTS = N_KEYS * N_KEYS
PEER_HEADS = 8
PEER_TOPK = 16
D_KEY = 256
EPS = 1e-6
NEG_BIG = -1e30

LANES = 128
ROW_TILE = 256
ATT_BLOCK = 256
TOPK_TOKENS = 128
PEER_TOKENS = 256
PEER_CHUNK = 2048
PEER_SUB = 512
PEER_BUILD_UNROLL = 8
W_PITCH = N_KEYS + 8
ATT_ROWS = 512
STICK_EXIT = -110.0
VMEM_LIMIT = 56 * 1024 * 1024

COL_AQ, COL_AK, COL_AV = 0, 4, 8
COL_BQ, COL_BK, COL_BV = 12, 16, 20


def _dot(a, b):
    return jnp.dot(a, b, preferred_element_type=F32)


def _dot_nt(a, b):
    return lax.dot_general(a, b, (((1,), (1,)), ((), ())), preferred_element_type=F32)


def _dot_tn(a, b):
    return lax.dot_general(a, b, (((0,), (0,)), ((), ())), preferred_element_type=F32)


def _params(sem):
    return pltpu.CompilerParams(dimension_semantics=sem, vmem_limit_bytes=VMEM_LIMIT)


def _qkv_body(x_ref, g_ref, w_ref, c_ref, s1_ref, s2_ref, ak_ref, av_ref, bk_ref, bv_ref, bf_ref):
    x = x_ref[...]
    xn = (x * lax.rsqrt(jnp.mean(x * x, axis=-1, keepdims=True) + EPS) * g_ref[...]).astype(BF16)
    cos_t = c_ref[...]
    sin_lo = s1_ref[...]
    sin_hi = s2_ref[...]
    f32_outs = {1: ak_ref, 2: av_ref, 4: bk_ref, 5: bv_ref}
    for grp in range(6):
        for j in range(4):
            cb = grp * 4 + j
            y = _dot(xn, w_ref[:, cb * LANES:(cb + 1) * LANES])
            if grp in (0, 1):
                y = (y * cos_t + pltpu.roll(y, LANES - ROT_DIM // 2, 1) * sin_lo
                     + pltpu.roll(y, ROT_DIM // 2, 1) * sin_hi)
            if grp in f32_outs:
                f32_outs[grp][:, j * LANES:(j + 1) * LANES] = y
            if grp in (0, 3):
                y = y * (HEAD_DIM ** -0.5)
            bf_ref[:, cb * LANES:(cb + 1) * LANES] = y.astype(BF16)


def _rope_tables(pos):
    half = ROT_DIM // 2
    inv_freq = ROPE_THETA ** (-jnp.arange(half, dtype=F32) * 2.0 / ROT_DIM)
    ang = pos.astype(F32)[:, None] * inv_freq[None, :]
    cos, sin = jnp.cos(ang), jnp.sin(ang)
    t = pos.shape[0]
    pad = jnp.zeros((t, HEAD_DIM - ROT_DIM), F32)
    zero = jnp.zeros((t, half), F32)
    cos_t = jnp.concatenate([cos, cos, pad + 1.0], axis=1)
    sin_lo = jnp.concatenate([-sin, zero, pad], axis=1)
    sin_hi = jnp.concatenate([zero, sin, pad], axis=1)
    tile = lambda a: jnp.concatenate([a, a], axis=1)
    return tile(cos_t), tile(sin_lo), tile(sin_hi)


def _qkv_project(x2d, pos, g, w_bf, tm):
    n = x2d.shape[0]
    t = pos.shape[0]
    per = t // tm
    tabs = _rope_tables(pos)
    row = lambda i: (i, 0)
    tab = lambda i: (i % per, 0)
    full = lambda i: (0, 0)
    f32_out = jax.ShapeDtypeStruct((n, A_WIDTH), F32)
    return pl.pallas_call(
        _qkv_body,
        grid=(n // tm,),
        in_specs=[pl.BlockSpec((tm, D_MODEL), row), pl.BlockSpec((1, D_MODEL), full),
                  pl.BlockSpec((D_MODEL, QKV_WIDTH), full),
                  pl.BlockSpec((tm, LANES), tab), pl.BlockSpec((tm, LANES), tab), pl.BlockSpec((tm, LANES), tab)],
        out_specs=[pl.BlockSpec((tm, A_WIDTH), row)] * 4 + [pl.BlockSpec((tm, QKV_WIDTH), row)],
        out_shape=[f32_out] * 4 + [jax.ShapeDtypeStruct((n, QKV_WIDTH), BF16)],
        compiler_params=_params(("parallel",)),
        name="qkv",
    )(x2d, g.reshape(1, D_MODEL), w_bf, *tabs)


def _stack_heads(q):
    lane = lax.broadcasted_iota(jnp.int32, q.shape, 1)
    zero = jnp.zeros_like(q)
    return jnp.concatenate([jnp.where(lane < HEAD_DIM, q, zero), jnp.where(lane >= HEAD_DIM, q, zero)], axis=0)


def _row_tiles(n2):
    sub = min(ATT_ROWS, n2)
    return [slice(r, r + sub) for r in range(0, n2, sub)]


def _tile_iotas(rows, n_keys, period):
    shape = (rows.stop - rows.start, n_keys)
    row = (lax.broadcasted_iota(jnp.int32, shape, 0) + rows.start) % period
    col = lax.broadcasted_iota(jnp.int32, shape, 1)
    return row, col


def _stack_heads_t(q):
    qt = q.astype(F32).T
    sub = lax.broadcasted_iota(jnp.int32, qt.shape, 0)
    first = jnp.where(sub < HEAD_DIM, qt, 0.0)
    second = jnp.where(sub >= HEAD_DIM, qt, 0.0)
    return jnp.concatenate([first, second], axis=1).astype(BF16)


def _softmax_init(n2):
    return jnp.full((1, n2), NEG_BIG, F32), jnp.zeros((1, n2), F32), jnp.zeros((LANES, n2), F32)


def _softmax_update(s, m, l):
    m_new = jnp.maximum(m, jnp.max(s, axis=0, keepdims=True))
    alpha = jnp.exp(m - m_new)
    p = jnp.exp(s - m_new)
    return m_new, alpha, alpha * l + jnp.sum(p, axis=0, keepdims=True), p.astype(BF16)


def _softmax_step(qst, kb, vb, mask, state):
    m, l, acc = state
    s = _dot(kb, qst)
    if mask is not None:
        s = jnp.where(mask, s, NEG_BIG)
    m, alpha, l, p = _softmax_update(s, m, l)
    return m, l, alpha * acc + _dot_tn(vb, p)


def _softmax_scan(qst, n_blocks, kb_fn, vb_fn, last_mask, state):
    m, l, acc = state

    def body(j, carry):
        s, p_prev, m, l, acc = carry
        s_next = _dot(kb_fn(j + 1), qst)
        pv_prev = _dot_tn(vb_fn(jnp.maximum(j - 1, 0)), p_prev)
        m, alpha, l, p = _softmax_update(s, m, l)
        return s_next, p, m, l, alpha * (acc + pv_prev)

    s0 = _dot(kb_fn(0), qst)
    last = n_blocks - 1
    s, p_prev, m, l, acc = lax.fori_loop(0, last, body, (s0, jnp.zeros(s0.shape, BF16), m, l, acc))
    if last_mask is not None:
        s = jnp.where(last_mask, s, NEG_BIG)
    pv_prev = _dot_tn(vb_fn(jnp.maximum(last - 1, 0)), p_prev)
    m, alpha, l, p = _softmax_update(s, m, l)
    return m, l, alpha * (acc + pv_prev) + _dot_tn(vb_fn(last), p)


def _diff_finish(state, lam_ref, g_ref, o_ref, lam_init):
    tq = o_ref.shape[0]
    _, l, acc = state
    lp = lam_ref[...]
    lam = (jnp.exp(jnp.sum(lp[0:1] * lp[1:2], axis=1, keepdims=True))
           - jnp.exp(jnp.sum(lp[2:3] * lp[3:4], axis=1, keepdims=True)) + lam_init)
    on = (acc / l).T
    o = on[:tq] - lam * on[tq:]
    o = o * lax.rsqrt(jnp.mean(o * o, axis=1, keepdims=True) + EPS) * g_ref[...] * (1.0 - lam_init)
    o_ref[...] = o.astype(o_ref.dtype)


def _stick_init(c_sc, acc_sc):
    c_sc[...] = jnp.zeros(c_sc.shape, F32)
    acc_sc[...] = jnp.zeros(acc_sc.shape, F32)


def _stick_step(qs_sc, kb, vb, tri, mask_fn, c_sc, acc_sc):
    for rows in _row_tiles(qs_sc.shape[0]):
        z = _dot_nt(qs_sc[rows, :], kb)
        lm = -(jnp.maximum(z, 0.0) + jnp.log1p(jnp.exp(-jnp.abs(z))))
        mask = None if mask_fn is None else mask_fn(rows)
        if mask is not None:
            lm = jnp.where(mask, lm, 0.0)
        hi = lm.astype(BF16)
        lo = (lm - hi.astype(F32)).astype(BF16)
        cum = _dot(hi, tri) + _dot(lo, tri)
        c = c_sc[rows, :]
        a = jnp.exp(z + cum + c)
        if mask is not None:
            a = jnp.where(mask, a, 0.0)
        acc_sc[rows, :] += _dot(a.astype(BF16), vb)
        c_sc[rows, :] = c + cum[:, 0:1]


def _tri(n):
    r = lax.broadcasted_iota(jnp.int32, (n, n), 0)
    c = lax.broadcasted_iota(jnp.int32, (n, n), 1)
    return jnp.where(r >= c, 1.0, 0.0).astype(BF16)


def _stick_finish(acc_sc, g_ref, o_ref):
    tq = o_ref.shape[0]
    lane = lax.broadcasted_iota(jnp.int32, (tq, LANES), 1)
    first = lane < HEAD_DIM
    o = jnp.where(first, acc_sc[:tq, :], acc_sc[tq:, :])
    sq = o * o
    ss0 = jnp.sum(jnp.where(first, sq, 0.0), axis=1, keepdims=True)
    ss1 = jnp.sum(jnp.where(first, 0.0, sq), axis=1, keepdims=True)
    ms = jnp.where(first, ss0, ss1) * (1.0 / HEAD_DIM)
    o_ref[...] = (o * lax.rsqrt(ms + EPS) * g_ref[...]).astype(o_ref.dtype)


def _stick_scan(n_blocks, step_fn, c_sc):
    def cond(st):
        n, cmax = st
        return jnp.logical_and(n < n_blocks, cmax > STICK_EXIT)

    def body(st):
        step_fn(st[0])
        return st[0] + 1, jnp.max(c_sc[...])

    return lax.while_loop(cond, body, (jnp.int32(0), jnp.max(c_sc[...])))[1]


def _stick_scratch(n2):
    return [pltpu.VMEM((n2, LANES), BF16), pltpu.VMEM((n2, 1), F32), pltpu.VMEM((n2, LANES), F32)]


def _diff_prompt_body(q_ref, k_ref, v_ref, km_ref, vm_ref, lam_ref, g_ref, o_ref, *, lam_init):
    tq = q_ref.shape[0]
    i = pl.program_id(2)
    qst = _stack_heads_t(q_ref[...])
    nm = km_ref.shape[0]
    key = lax.broadcasted_iota(jnp.int32, (nm, 2 * tq), 0)
    state = _softmax_step(qst, km_ref[...], vm_ref[...], key < N_META, _softmax_init(2 * tq))
    key = lax.broadcasted_iota(jnp.int32, (tq, 2 * tq), 0)
    query = lax.broadcasted_iota(jnp.int32, (tq, 2 * tq), 1) % tq
    block = lambda ref: (lambda j: ref[pl.ds(pl.multiple_of(j * tq, tq), tq), :])
    state = _softmax_scan(qst, i + 1, block(k_ref), block(v_ref), (key // CHUNK) <= (query // CHUNK), state)
    _diff_finish(state, lam_ref, g_ref, o_ref, lam_init)


def _stick_prompt_body(q_ref, k_ref, v_ref, km_ref, vm_ref, g_ref, o_ref, qs_sc, c_sc, acc_sc):
    tq = q_ref.shape[0]
    i = pl.program_id(2)
    qs_sc[...] = _stack_heads(q_ref[...])
    _stick_init(c_sc, acc_sc)
    tri = _tri(tq)

    def causal_mask(rows):
        row, col = _tile_iotas(rows, tq, tq)
        return col < row

    off = pl.multiple_of(i * tq, tq)
    _stick_step(qs_sc, k_ref[pl.ds(off, tq), :], v_ref[pl.ds(off, tq), :], tri, causal_mask, c_sc, acc_sc)

    def step(n):
        o = pl.multiple_of((i - 1 - n) * tq, tq)
        _stick_step(qs_sc, k_ref[pl.ds(o, tq), :], v_ref[pl.ds(o, tq), :], tri, None, c_sc, acc_sc)

    cmax = _stick_scan(i, step, c_sc)
    nm = km_ref.shape[0]

    @pl.when(cmax > STICK_EXIT)
    def _():
        _stick_step(qs_sc, km_ref[...], vm_ref[...], tri[:nm, :nm],
                    lambda rows: _tile_iotas(rows, nm, tq)[1] < N_META, c_sc, acc_sc)

    _stick_finish(acc_sc, g_ref, o_ref)


def _prompt_attention(qkv_bf, meta_bf, batch, seq, diff_lambda, a_g, b_g, lam_init):
    tq = ATT_BLOCK
    nq = seq // tq
    n = batch * seq
    qmap = lambda off: (lambda b, h, i: (b * nq + i, off + h))
    kvmap = lambda off: (lambda b, h, i: (b, off + h))
    mmap = lambda off: (lambda b, h, i: (0, off + h))
    full = lambda b, h, i: (0, 0)
    nm = meta_bf.shape[0]

    def specs(cq, ck, cv):
        return [pl.BlockSpec((tq, LANES), qmap(cq)),
                pl.BlockSpec((seq, LANES), kvmap(ck)), pl.BlockSpec((seq, LANES), kvmap(cv)),
                pl.BlockSpec((nm, LANES), mmap(ck)), pl.BlockSpec((nm, LANES), mmap(cv))]

    out_spec = pl.BlockSpec((tq, LANES), lambda b, h, i: (b * nq + i, h))
    out_shape = jax.ShapeDtypeStruct((n, A_WIDTH), BF16)
    sem = ("parallel", "parallel", "arbitrary")
    a_out = pl.pallas_call(
        functools.partial(_diff_prompt_body, lam_init=lam_init),
        grid=(batch, A_HEADS, nq),
        in_specs=specs(COL_AQ, COL_AK, COL_AV) + [pl.BlockSpec((4, HEAD_DIM), full), pl.BlockSpec((1, LANES), full)],
        out_specs=out_spec, out_shape=out_shape,
        compiler_params=_params(sem), name="diff_prompt",
    )(qkv_bf, qkv_bf, qkv_bf, meta_bf, meta_bf, diff_lambda, a_g.reshape(1, LANES))
    b_g2 = jnp.concatenate([b_g, b_g]).reshape(1, LANES)
    b_out = pl.pallas_call(
        _stick_prompt_body,
        grid=(batch, B_HEADS // 2, nq),
        in_specs=specs(COL_BQ, COL_BK, COL_BV) + [pl.BlockSpec((1, LANES), full)],
        out_specs=out_spec, out_shape=out_shape, scratch_shapes=_stick_scratch(2 * tq),
        compiler_params=_params(sem), name="stick_prompt",
    )(qkv_bf, qkv_bf, qkv_bf, meta_bf, meta_bf, b_g2)
    return a_out, b_out


def _last_block(tail_ref, new_ref):
    tail = tail_ref[...].astype(BF16)
    new = new_ref[...]
    pad = jnp.zeros((LANES - tail.shape[0] - new.shape[0], LANES), BF16)
    return jnp.concatenate([tail, new, pad], axis=0)


def _diff_sample_body(q_ref, kn_ref, vn_ref, kc_ref, vc_ref, kt_ref, vt_ref, lam_ref, g_ref, o_ref, *, lam_init):
    ts = q_ref.shape[0]
    qst = _stack_heads_t(q_ref[...])
    key = lax.broadcasted_iota(jnp.int32, (LANES, 2 * ts), 0)
    state = _softmax_step(qst, _last_block(kt_ref, kn_ref), _last_block(vt_ref, vn_ref),
                          key < kt_ref.shape[0] + ts, _softmax_init(2 * ts))
    blk = ATT_BLOCK
    block = lambda ref: (lambda j: ref[pl.ds(pl.multiple_of(j * blk, blk), blk), :].astype(BF16))
    state = _softmax_scan(qst, kc_ref.shape[0] // blk, block(kc_ref), block(vc_ref), None, state)
    _diff_finish(state, lam_ref, g_ref, o_ref, lam_init)


def _stick_sample_body(q_ref, kn_ref, vn_ref, kc_ref, vc_ref, kt_ref, vt_ref, g_ref, o_ref, qs_sc, c_sc, acc_sc):
    ts = q_ref.shape[0]
    nt = kt_ref.shape[0]
    qs_sc[...] = _stack_heads(q_ref[...])
    _stick_init(c_sc, acc_sc)
    blk = ATT_BLOCK
    tri = _tri(blk)

    def last_mask(rows):
        row, col = _tile_iotas(rows, LANES, ts)
        return (col < nt) | ((col < nt + ts) & (col - nt < row))

    _stick_step(qs_sc, _last_block(kt_ref, kn_ref), _last_block(vt_ref, vn_ref), tri[:LANES, :LANES], last_mask,
                c_sc, acc_sc)
    nb = kc_ref.shape[0] // blk

    def step(n):
        off = pl.multiple_of((nb - 1 - n) * blk, blk)
        _stick_step(qs_sc, kc_ref[pl.ds(off, blk), :].astype(BF16), vc_ref[pl.ds(off, blk), :].astype(BF16),
                    tri, None, c_sc, acc_sc)

    _stick_scan(nb, step, c_sc)
    _stick_finish(acc_sc, g_ref, o_ref)


def _sample_attention(qkv_bf, ca_k, ca_v, cb_k, cb_v, dec_batch, dec_seq, diff_lambda, a_g, b_g, lam_init):
    n_past = ca_k.shape[1]
    main = n_past - N_META
    tail_blk = main // N_META
    full = lambda b, h: (0, 0)

    def specs(cq, ck, cv):
        return [pl.BlockSpec((dec_seq, LANES), lambda b, h: (b, cq + h)),
                pl.BlockSpec((dec_seq, LANES), lambda b, h: (b, ck + h)),
                pl.BlockSpec((dec_seq, LANES), lambda b, h: (b, cv + h)),
                pl.BlockSpec((None, main, LANES), lambda b, h: (b, 0, h)),
                pl.BlockSpec((None, main, LANES), lambda b, h: (b, 0, h)),
                pl.BlockSpec((None, N_META, LANES), lambda b, h: (b, tail_blk, h)),
                pl.BlockSpec((None, N_META, LANES), lambda b, h: (b, tail_blk, h))]

    out_spec = pl.BlockSpec((dec_seq, LANES), lambda b, h: (b, h))
    out_shape = jax.ShapeDtypeStruct((dec_batch * dec_seq, A_WIDTH), BF16)
    sem = ("parallel", "parallel")
    a_out = pl.pallas_call(
        functools.partial(_diff_sample_body, lam_init=lam_init),
        grid=(dec_batch, A_HEADS),
        in_specs=specs(COL_AQ, COL_AK, COL_AV) + [pl.BlockSpec((4, HEAD_DIM), full), pl.BlockSpec((1, LANES), full)],
        out_specs=out_spec, out_shape=out_shape,
        compiler_params=_params(sem), name="diff_sample",
    )(qkv_bf, qkv_bf, qkv_bf, ca_k, ca_v, ca_k, ca_v, diff_lambda, a_g.reshape(1, LANES))
    b_g2 = jnp.concatenate([b_g, b_g]).reshape(1, LANES)
    b_out = pl.pallas_call(
        _stick_sample_body,
        grid=(dec_batch, B_HEADS // 2),
        in_specs=specs(COL_BQ, COL_BK, COL_BV) + [pl.BlockSpec((1, LANES), full)],
        out_specs=out_spec, out_shape=out_shape, scratch_shapes=_stick_scratch(2 * dec_seq),
        compiler_params=_params(sem), name="stick_sample",
    )(qkv_bf, qkv_bf, qkv_bf, cb_k, cb_v, cb_k, cb_v, b_g2)
    return a_out, b_out


def _post_body(x_ref, a_ref, b_ref, wo_ref, g_ref, wq_ref, sk_ref, h_ref, hn_ref, s_ref):
    h = (x_ref[...] + _dot(a_ref[...], wo_ref[:A_WIDTH, :]) + _dot(b_ref[...], wo_ref[A_WIDTH:, :]))
    h_ref[...] = h
    hn = (h * lax.rsqrt(jnp.mean(h * h, axis=-1, keepdims=True) + EPS) * g_ref[...]).astype(BF16)
    hn_ref[...] = hn
    half = D_KEY // 2
    for hp in range(2 * PEER_HEADS):
        q = _dot(hn, wq_ref[:, hp * half:(hp + 1) * half]).astype(BF16)
        s_ref[hp] = _dot_nt(sk_ref[hp % 2], q)


def _post_attention(x2d, a_out, b_out, wo_bf, g_ffn, wq_bf, sk_bf, tm):
    n = x2d.shape[0]
    row = lambda i: (i, 0)
    full = lambda i: (0, 0)
    return pl.pallas_call(
        _post_body,
        grid=(n // tm,),
        in_specs=[pl.BlockSpec((tm, D_MODEL), row), pl.BlockSpec((tm, A_WIDTH), row), pl.BlockSpec((tm, B_WIDTH), row),
                  pl.BlockSpec((MIX_WIDTH, D_MODEL), full), pl.BlockSpec((1, D_MODEL), full),
                  pl.BlockSpec((D_MODEL, PEER_HEADS * D_KEY), full),
                  pl.BlockSpec((2, N_KEYS, D_KEY // 2), lambda i: (0, 0, 0))],
        out_specs=[pl.BlockSpec((tm, D_MODEL), row), pl.BlockSpec((tm, D_MODEL), row),
                   pl.BlockSpec((2 * PEER_HEADS, N_KEYS, tm), lambda i: (0, 0, i))],
        out_shape=[jax.ShapeDtypeStruct((n, D_MODEL), F32), jax.ShapeDtypeStruct((n, D_MODEL), BF16),
                   jax.ShapeDtypeStruct((2 * PEER_HEADS, N_KEYS, n), F32)],
        compiler_params=_params(("parallel",)),
        name="post",
    )(x2d, a_out, b_out, wo_bf, g_ffn.reshape(1, D_MODEL), wq_bf, sk_bf)


def _extract_top(s, iota, take, val_ref, aux, aux_ref):
    n_rows = float(s.shape[0])
    for r in range(take):
        m = jnp.max(s, axis=0, keepdims=True)
        pos = jnp.min(jnp.where(s == m, iota, n_rows), axis=0, keepdims=True)
        sel = iota == pos
        val_ref[r:r + 1, :] = m
        if aux is None:
            aux_ref[r:r + 1, :] = pos
        else:
            aux_ref[r:r + 1, :] = jnp.sum(jnp.where(sel, aux, 0.0), axis=0, keepdims=True)
        s = jnp.where(sel, -jnp.inf, s)


def _topk_body(s_ref, row_ref, col_ref, gate_ref, v0_ref, i0_ref, v1_ref, i1_ref, bs_ref, id_ref,
               rows_sc, cols_sc, gates_sc):
    tt = s_ref.shape[2]
    k = PEER_TOPK
    key_iota = lax.broadcasted_iota(jnp.int32, (N_KEYS, tt), 0).astype(F32)
    n_cand = k + 8 * 7 + 8
    cand_iota = lax.broadcasted_iota(jnp.int32, (n_cand, tt), 0).astype(F32)

    def head(h, carry):
        _extract_top(s_ref[2 * h], key_iota, k, v0_ref, None, i0_ref)
        _extract_top(s_ref[2 * h + 1], key_iota, k, v1_ref, None, i1_ref)
        v1_all, i1_all = v1_ref[...], i1_ref[...]
        cs = [v0_ref[0:1, :] + v1_all]
        ci = [i0_ref[0:1, :] * N_KEYS + i1_all]
        for r in range(1, 8):
            cs.append(v0_ref[r:r + 1, :] + v1_all[:8])
            ci.append(i0_ref[r:r + 1, :] * N_KEYS + i1_all[:8])
        cs.append(v0_ref[8:16, :] + v1_all[0:1])
        ci.append(i0_ref[8:16, :] * N_KEYS + i1_all[0:1])
        _extract_top(jnp.concatenate(cs, axis=0), cand_iota, k, bs_ref, jnp.concatenate(ci, axis=0), id_ref)
        best = bs_ref[...]
        e = jnp.exp(best - best[0:1])
        ids = id_ref[...]
        rows = jnp.floor(ids * (1.0 / N_KEYS))
        dst = pl.ds(pl.multiple_of(h * k, k), k)
        gates_sc[dst, :] = e / jnp.sum(e, axis=0, keepdims=True)
        rows_sc[dst, :] = rows
        cols_sc[dst, :] = ids - rows * N_KEYS
        return carry

    lax.fori_loop(0, PEER_HEADS, head, 0)
    row_ref[...] = rows_sc[...].T
    col_ref[...] = cols_sc[...].T
    gate_ref[...] = gates_sc[...].T


def _peer_topk(scores):
    n = scores.shape[2]
    tt = TOPK_TOKENS
    slots = PEER_HEADS * PEER_TOPK
    out = jax.ShapeDtypeStruct((n, slots), F32)
    return pl.pallas_call(
        _topk_body,
        grid=(n // tt,),
        in_specs=[pl.BlockSpec((2 * PEER_HEADS, N_KEYS, tt), lambda i: (0, 0, i))],
        out_specs=[pl.BlockSpec((tt, slots), lambda i: (i, 0))] * 3,
        out_shape=[out] * 3,
        scratch_shapes=[pltpu.VMEM((PEER_TOPK, tt), F32)] * 6 + [pltpu.VMEM((slots, tt), F32)] * 3,
        compiler_params=_params(("parallel",)),
        name="topk",
    )(scores)


def _gelu_x2(x):
    return x * (1.0 + lax.erf(x * (2.0 ** -0.5)))


def _peer_body(hn_ref, row_ref, col_ref, gate_ref, ut_ref, v_ref, h_ref, gf_ref, y_ref, w_sc, acc_sc):
    tb = hn_ref.shape[0]
    c = pl.program_id(1)
    per_chunk = v_ref.shape[0] // N_KEYS

    @pl.when(c == 0)
    def _():
        acc_sc[...] = jnp.zeros_like(acc_sc)
        sub = lax.broadcasted_iota(jnp.int32, (N_KEYS, LANES), 0).astype(F32)

        def build(t, carry):
            r = row_ref[pl.ds(t, 1), :]
            cc = col_ref[pl.ds(t, 1), :]
            g = gate_ref[pl.ds(t, 1), :] * 0.5
            at = jnp.where(sub == r, g, 0.0).astype(BF16)
            bt = jnp.where(sub == cc, 1.0, 0.0).astype(BF16)
            w_sc[pl.ds(pl.multiple_of(t * W_PITCH, 8), N_KEYS), :] = _dot_nt(at, bt)
            return carry

        lax.fori_loop(0, tb, build, 0, unroll=PEER_BUILD_UNROLL)

    x = hn_ref[...]
    total = None
    for s in range(v_ref.shape[0] // PEER_SUB):
        experts = slice(s * PEER_SUB, (s + 1) * PEER_SUB)
        pre = _dot(x, ut_ref[:, experts])
        first = c * per_chunk + s * (PEER_SUB // N_KEYS)
        w = jnp.concatenate([w_sc[pl.ds(first + a, tb, stride=W_PITCH), :] for a in range(PEER_SUB // N_KEYS)],
                            axis=1)
        part = _dot((_gelu_x2(pre) * w).astype(BF16), v_ref[experts, :])
        total = part if total is None else total + part
    acc_sc[...] += total

    @pl.when(c == pl.num_programs(1) - 1)
    def _():
        h = h_ref[...] + acc_sc[...]
        y_ref[...] = h * lax.rsqrt(jnp.mean(h * h, axis=-1, keepdims=True) + EPS) * gf_ref[...]


def _peer_dense(hn, rows, cols, gates, ut_bf, v_bf, h, g_final):
    n = hn.shape[0]
    tb = PEER_TOKENS
    slots = PEER_HEADS * PEER_TOPK
    tok = lambda i, c: (i, 0)
    chunk = lambda i, c: (c, 0)
    return pl.pallas_call(
        _peer_body,
        grid=(n // tb, N_EXPERTS // PEER_CHUNK),
        in_specs=[pl.BlockSpec((tb, D_MODEL), tok), pl.BlockSpec((tb, slots), tok), pl.BlockSpec((tb, slots), tok),
                  pl.BlockSpec((tb, slots), tok), pl.BlockSpec((D_MODEL, PEER_CHUNK), lambda i, c: (0, c)),
                  pl.BlockSpec((PEER_CHUNK, D_MODEL), chunk), pl.BlockSpec((tb, D_MODEL), tok),
                  pl.BlockSpec((1, D_MODEL), lambda i, c: (0, 0))],
        out_specs=pl.BlockSpec((tb, D_MODEL), tok),
        out_shape=jax.ShapeDtypeStruct((n, D_MODEL), F32),
        scratch_shapes=[pltpu.VMEM((tb * W_PITCH, LANES), F32), pltpu.VMEM((tb, D_MODEL), F32)],
        compiler_params=_params(("parallel", "arbitrary")),
        name="peer",
    )(hn, rows, cols, gates, ut_bf, v_bf, h, g_final.reshape(1, D_MODEL))


def _finish(x2d, a_out, b_out, wo_bf, g_ffn, wq_bf, sk_bf, u_bf, v_bf, g_final):
    h, hn, scores = _post_attention(x2d, a_out, b_out, wo_bf, g_ffn, wq_bf, sk_bf, ROW_TILE)
    rows, cols, gates = _peer_topk(scores)
    return _peer_dense(hn, rows, cols, gates, u_bf, v_bf, h, g_final)


def kernel(x_prompt, x_sample, cache_a_k, cache_a_v, cache_b_k, cache_b_v, meta_tokens, g_attn, w_qkv,
           diff_lambda, a_norm_g, b_norm_g, w_o, g_ffn, w_peer_q, peer_sub_keys, peer_u, peer_v, g_final):
    batch, seq, _ = x_prompt.shape
    dec_batch, dec_seq, _ = x_sample.shape
    depth = w_qkv.shape[0]
    assert depth == 1, "single-layer step"
    n_past = cache_a_k.shape[2]
    lam_init = 0.8 - 0.6 * math.exp(-0.3 * 0)

    w_bf = w_qkv[0].astype(BF16)
    wo_bf = w_o[0].astype(BF16)
    wq_bf = w_peer_q[0].astype(BF16)
    sk_bf = peer_sub_keys[0].astype(BF16)
    u_bf = peer_u[0].astype(BF16).T
    v_bf = peer_v[0].astype(BF16)
    g1, g2 = g_attn[0], g_ffn[0]
    lam_p, a_g, b_g = diff_lambda[0], a_norm_g[0], b_norm_g[0]

    xp = x_prompt.reshape(batch * seq, D_MODEL)
    xs = x_sample.reshape(dec_batch * dec_seq, D_MODEL)
    p_ak, p_av, p_bk, p_bv, p_bf = _qkv_project(xp, N_META + jnp.arange(seq, dtype=jnp.int32), g1, w_bf, ROW_TILE)
    m_ak, m_av, m_bk, m_bv, m_bf = _qkv_project(meta_tokens.astype(F32), jnp.arange(N_META, dtype=jnp.int32),
                                                g1, w_bf, N_META)
    s_ak, s_av, s_bk, s_bv, s_bf = _qkv_project(xs, n_past + jnp.arange(dec_seq, dtype=jnp.int32), g1, w_bf, dec_seq)

    meta_pad = jnp.pad(m_bf, ((0, LANES - N_META), (0, 0)))
    pa, pb = _prompt_attention(p_bf, meta_pad, batch, seq, lam_p, a_g, b_g, lam_init)
    y_prompt = _finish(xp, pa, pb, wo_bf, g2, wq_bf, sk_bf, u_bf, v_bf, g_final)

    ca_k = cache_a_k[0].reshape(dec_batch, n_past, A_WIDTH)
    ca_v = cache_a_v[0].reshape(dec_batch, n_past, A_WIDTH)
    cb_k = cache_b_k[0].reshape(dec_batch, n_past, B_WIDTH)
    cb_v = cache_b_v[0].reshape(dec_batch, n_past, B_WIDTH)
    sa, sb = _sample_attention(s_bf, ca_k, ca_v, cb_k, cb_v, dec_batch, dec_seq, lam_p, a_g, b_g, lam_init)
    y_sample = _finish(xs, sa, sb, wo_bf, g2, wq_bf, sk_bf, u_bf, v_bf, g_final)

    def prompt_cache(meta_rows, frame_rows, heads):
        m = jnp.broadcast_to(meta_rows[None], (batch, N_META, A_WIDTH))
        full = jnp.concatenate([m, frame_rows.reshape(batch, seq, A_WIDTH)], axis=1)
        return full.reshape(1, batch, N_META + seq, heads, A_WIDTH // heads)

    def sample_cache(rows, heads):
        return rows.reshape(1, dec_batch, dec_seq, heads, A_WIDTH // heads)

    return (y_prompt.reshape(batch, seq, D_MODEL), y_sample.reshape(dec_batch, dec_seq, D_MODEL),
            prompt_cache(m_ak, p_ak, A_HEADS), prompt_cache(m_av, p_av, A_HEADS),
            prompt_cache(m_bk, p_bk, B_HEADS), prompt_cache(m_bv, p_bv, B_HEADS),
            sample_cache(s_ak, A_HEADS), sample_cache(s_av, A_HEADS),
            sample_cache(s_bk, B_HEADS), sample_cache(s_bv, B_HEADS))
```

```python
import functools
import math

import jax
import jax.numpy as jnp
from jax import lax
from jax.experimental import pallas as pl
from jax.experimental.pallas import tpu as pltpu

F32 = jnp.float32
BF16 = jnp.bfloat16

D_MODEL = 1024
CHUNK = 64
N_META = 16
HEAD_DIM = 64
A_HEADS = 4
B_HEADS = 8
A_WIDTH = A_HEADS * 2 * HEAD_DIM
B_WIDTH = B_HEADS * HEAD_DIM
MIX_WIDTH = A_WIDTH + B_WIDTH
QKV_WIDTH = 3 * MIX_WIDTH
ROT_DIM = HEAD_DIM // 4
ROPE_THETA = 500000.0
N_KEYS = 128
N_EXPERTS = N_KEYS * N_KEYS
PEER_HEADS = 8
PEER_TOPK = 16
D_KEY = 256
EPS = 1e-6
NEG_BIG = -1e30

LANES = 128
ROW_TILE = 256
POST_TILE = 512
ATT_BLOCK = 256
TOPK_TOKENS = 128
PEER_TOKENS = 256
PEER_CHUNK = 2048
PEER_SUB = 512
PEER_BUILD_UNROLL = 8
W_PITCH = N_KEYS + 8
STICK_EXIT = -110.0
VMEM_LIMIT = 56 * 1024 * 1024

COL_AQ, COL_AK, COL_AV = 0, 4, 8
COL_BQ, COL_BK, COL_BV = 12, 16, 20


def _dot(a, b):
    return jnp.dot(a, b, preferred_element_type=F32)


def _dot_nt(a, b):
    return lax.dot_general(a, b, (((1,), (1,)), ((), ())), preferred_element_type=F32)


def _dot_tn(a, b):
    return lax.dot_general(a, b, (((0,), (0,)), ((), ())), preferred_element_type=F32)


def _params(sem):
    return pltpu.CompilerParams(dimension_semantics=sem, vmem_limit_bytes=VMEM_LIMIT)


def _qkv_body(x_ref, g_ref, w_ref, c_ref, s1_ref, s2_ref, ak_ref, av_ref, bk_ref, bv_ref, bf_ref):
    x = x_ref[...]
    xn = (x * lax.rsqrt(jnp.mean(x * x, axis=-1, keepdims=True) + EPS) * g_ref[...]).astype(BF16)
    cos_t = c_ref[...]
    sin_lo = s1_ref[...]
    sin_hi = s2_ref[...]
    f32_outs = {1: ak_ref, 2: av_ref, 4: bk_ref, 5: bv_ref}
    for grp in range(6):
        for half in range(2):
            y2 = _dot(xn, w_ref[:, (grp * 4 + half * 2) * LANES:(grp * 4 + half * 2 + 2) * LANES])
            for k in range(2):
                j = half * 2 + k
                cb = grp * 4 + j
                y = y2[:, k * LANES:(k + 1) * LANES]
                if grp in (0, 1):
                    y = (y * cos_t + pltpu.roll(y, LANES - ROT_DIM // 2, 1) * sin_lo
                         + pltpu.roll(y, ROT_DIM // 2, 1) * sin_hi)
                if grp in f32_outs:
                    f32_outs[grp][:, j * LANES:(j + 1) * LANES] = y
                if grp in (0, 3):
                    y = y * (HEAD_DIM ** -0.5)
                bf_ref[:, cb * LANES:(cb + 1) * LANES] = y.astype(BF16)


def _rope_tables(pos):
    half = ROT_DIM // 2
    inv_freq = ROPE_THETA ** (-jnp.arange(half, dtype=F32) * 2.0 / ROT_DIM)
    ang = pos.astype(F32)[:, None] * inv_freq[None, :]
    cos, sin = jnp.cos(ang), jnp.sin(ang)
    t = pos.shape[0]
    pad = jnp.zeros((t, HEAD_DIM - ROT_DIM), F32)
    zero = jnp.zeros((t, half), F32)
    cos_t = jnp.concatenate([cos, cos, pad + 1.0], axis=1)
    sin_lo = jnp.concatenate([-sin, zero, pad], axis=1)
    sin_hi = jnp.concatenate([zero, sin, pad], axis=1)
    tile = lambda a: jnp.concatenate([a, a], axis=1)
    return tile(cos_t), tile(sin_lo), tile(sin_hi)


def _qkv_project(x2d, pos, g, w_bf, tm):
    n = x2d.shape[0]
    t = pos.shape[0]
    per = t // tm
    tabs = _rope_tables(pos)
    row = lambda i: (i, 0)
    tab = lambda i: (i % per, 0)
    full = lambda i: (0, 0)
    f32_out = jax.ShapeDtypeStruct((n, A_WIDTH), F32)
    return pl.pallas_call(
        _qkv_body,
        grid=(n // tm,),
        in_specs=[pl.BlockSpec((tm, D_MODEL), row), pl.BlockSpec((1, D_MODEL), full),
                  pl.BlockSpec((D_MODEL, QKV_WIDTH), full),
                  pl.BlockSpec((tm, LANES), tab), pl.BlockSpec((tm, LANES), tab), pl.BlockSpec((tm, LANES), tab)],
        out_specs=[pl.BlockSpec((tm, A_WIDTH), row)] * 4 + [pl.BlockSpec((tm, QKV_WIDTH), row)],
        out_shape=[f32_out] * 4 + [jax.ShapeDtypeStruct((n, QKV_WIDTH), BF16)],
        compiler_params=_params(("parallel",)),
        name="qkv",
    )(x2d, g.reshape(1, D_MODEL), w_bf, *tabs)


def _stack_heads_t(q):
    qt = q.astype(F32).T
    sub = lax.broadcasted_iota(jnp.int32, qt.shape, 0)
    first = jnp.where(sub < HEAD_DIM, qt, 0.0)
    second = jnp.where(sub >= HEAD_DIM, qt, 0.0)
    return jnp.concatenate([first, second], axis=1).astype(BF16)


def _softmax_init(n2):
    return jnp.full((1, n2), NEG_BIG, F32), jnp.zeros((1, n2), F32), jnp.zeros((LANES, n2), F32)


def _softmax_update(s, m, l):
    m_new = jnp.maximum(m, jnp.max(s, axis=0, keepdims=True))
    alpha = jnp.exp(m - m_new)
    p = jnp.exp(s - m_new)
    return m_new, alpha, alpha * l + jnp.sum(p, axis=0, keepdims=True), p.astype(BF16)


def _softmax_step(qst, kb, vb, mask, state):
    m, l, acc = state
    s = _dot(kb, qst)
    if mask is not None:
        s = jnp.where(mask, s, NEG_BIG)
    m, alpha, l, p = _softmax_update(s, m, l)
    return m, l, alpha * acc + _dot_tn(vb, p)


def _softmax_scan(qst, n_blocks, kb_fn, vb_fn, last_mask, state):
    m, l, acc = state

    def body(j, carry):
        s, p_prev, m, l, acc = carry
        s_next = _dot(kb_fn(j + 1), qst)
        pv_prev = _dot_tn(vb_fn(jnp.maximum(j - 1, 0)), p_prev)
        m, alpha, l, p = _softmax_update(s, m, l)
        return s_next, p, m, l, alpha * (acc + pv_prev)

    s0 = _dot(kb_fn(0), qst)
    last = n_blocks - 1
    s, p_prev, m, l, acc = lax.fori_loop(0, last, body, (s0, jnp.zeros(s0.shape, BF16), m, l, acc))
    if last_mask is not None:
        s = jnp.where(last_mask, s, NEG_BIG)
    pv_prev = _dot_tn(vb_fn(jnp.maximum(last - 1, 0)), p_prev)
    m, alpha, l, p = _softmax_update(s, m, l)
    return m, l, alpha * (acc + pv_prev) + _dot_tn(vb_fn(last), p)


def _diff_finish(state, lam_ref, g_ref, o_ref, lam_init):
    tq = o_ref.shape[0]
    _, l, acc = state
    lp = lam_ref[...]
    lam = (jnp.exp(jnp.sum(lp[0:1] * lp[1:2], axis=1, keepdims=True))
           - jnp.exp(jnp.sum(lp[2:3] * lp[3:4], axis=1, keepdims=True)) + lam_init)
    on = (acc / l).T
    o = on[:tq] - lam * on[tq:]
    o = o * lax.rsqrt(jnp.mean(o * o, axis=1, keepdims=True) + EPS) * g_ref[...] * (1.0 - lam_init)
    o_ref[...] = o.astype(o_ref.dtype)


def _stick_init(n2):
    return jnp.zeros((1, n2), F32), jnp.zeros((LANES, n2), F32)


def _stick_step(qst, kb, vb, tri, mask, state, kv_transposed=False):
    c, acc = state
    z = _dot_tn(kb, qst) if kv_transposed else _dot(kb, qst)
    lm = -(jnp.maximum(z, 0.0) + jnp.log1p(jnp.exp(-jnp.abs(z))))
    if mask is not None:
        lm = jnp.where(mask, lm, 0.0)
    hi = lm.astype(BF16)
    lo = (lm - hi.astype(F32)).astype(BF16)
    cum = _dot(tri, hi) + _dot(tri, lo)
    a = jnp.exp(z + cum + c)
    if mask is not None:
        a = jnp.where(mask, a, 0.0)
    a = a.astype(BF16)
    return c + cum[0:1, :], acc + (_dot(vb, a) if kv_transposed else _dot_tn(vb, a))


def _tri(n):
    r = lax.broadcasted_iota(jnp.int32, (n, n), 0)
    c = lax.broadcasted_iota(jnp.int32, (n, n), 1)
    return jnp.where(c >= r, 1.0, 0.0).astype(BF16)


def _stick_finish(state, g_ref, o_ref):
    tq = o_ref.shape[0]
    acc = state[1].T
    lane = lax.broadcasted_iota(jnp.int32, (tq, LANES), 1)
    first = lane < HEAD_DIM
    o = jnp.where(first, acc[:tq], acc[tq:])
    sq = o * o
    ss0 = jnp.sum(jnp.where(first, sq, 0.0), axis=1, keepdims=True)
    ss1 = jnp.sum(jnp.where(first, 0.0, sq), axis=1, keepdims=True)
    ms = jnp.where(first, ss0, ss1) * (1.0 / HEAD_DIM)
    o_ref[...] = (o * lax.rsqrt(ms + EPS) * g_ref[...]).astype(o_ref.dtype)


def _stick_scan(n_blocks, step_fn, state):
    def cond(st):
        return jnp.logical_and(st[0] < n_blocks, st[1] > STICK_EXIT)

    def body(st):
        state = step_fn(st[0], st[2])
        return st[0] + 1, jnp.max(state[0]), state

    out = lax.while_loop(cond, body, (jnp.int32(0), jnp.max(state[0]), state))
    return out[1], out[2]


def _diff_prompt_body(q_ref, k_ref, v_ref, km_ref, vm_ref, lam_ref, g_ref, o_ref, *, lam_init):
    tq = q_ref.shape[0]
    i = pl.program_id(2)
    qst = _stack_heads_t(q_ref[...])
    nm = km_ref.shape[0]
    key = lax.broadcasted_iota(jnp.int32, (nm, 2 * tq), 0)
    state = _softmax_step(qst, km_ref[...], vm_ref[...], key < N_META, _softmax_init(2 * tq))
    key = lax.broadcasted_iota(jnp.int32, (tq, 2 * tq), 0)
    query = lax.broadcasted_iota(jnp.int32, (tq, 2 * tq), 1) % tq
    block = lambda ref: (lambda j: ref[pl.ds(pl.multiple_of(j * tq, tq), tq), :])
    state = _softmax_scan(qst, i + 1, block(k_ref), block(v_ref), (key // CHUNK) <= (query // CHUNK), state)
    _diff_finish(state, lam_ref, g_ref, o_ref, lam_init)


def _stick_prompt_body(q_ref, k_ref, v_ref, km_ref, vm_ref, g_ref, o_ref):
    tq = q_ref.shape[0]
    i = pl.program_id(2)
    qst = _stack_heads_t(q_ref[...])
    tri = _tri(tq)
    block = lambda ref, j: ref[pl.ds(pl.multiple_of(j * tq, tq), tq), :]
    key = lax.broadcasted_iota(jnp.int32, (tq, 2 * tq), 0)
    query = lax.broadcasted_iota(jnp.int32, (tq, 2 * tq), 1) % tq
    state = _stick_step(qst, block(k_ref, i), block(v_ref, i), tri, key < query, _stick_init(2 * tq))
    cmax, state = _stick_scan(
        i, lambda n, st: _stick_step(qst, block(k_ref, i - 1 - n), block(v_ref, i - 1 - n), tri, None, st), state)
    nm = km_ref.shape[0]
    meta = lax.broadcasted_iota(jnp.int32, (nm, 2 * tq), 0) < N_META
    state = lax.cond(cmax > STICK_EXIT,
                     lambda st: _stick_step(qst, km_ref[...], vm_ref[...], tri[:nm, :nm], meta, st),
                     lambda st: st, state)
    _stick_finish(state, g_ref, o_ref)


def _prompt_attention(qkv_bf, meta_bf, batch, seq, diff_lambda, a_g, b_g, lam_init):
    tq = ATT_BLOCK
    nq = seq // tq
    n = batch * seq
    qmap = lambda off: (lambda b, h, i: (b * nq + i, off + h))
    kvmap = lambda off: (lambda b, h, i: (b, off + h))
    mmap = lambda off: (lambda b, h, i: (0, off + h))
    full = lambda b, h, i: (0, 0)
    nm = meta_bf.shape[0]

    def specs(cq, ck, cv):
        return [pl.BlockSpec((tq, LANES), qmap(cq)),
                pl.BlockSpec((seq, LANES), kvmap(ck)), pl.BlockSpec((seq, LANES), kvmap(cv)),
                pl.BlockSpec((nm, LANES), mmap(ck)), pl.BlockSpec((nm, LANES), mmap(cv))]

    out_spec = pl.BlockSpec((tq, LANES), lambda b, h, i: (b * nq + i, h))
    out_shape = jax.ShapeDtypeStruct((n, A_WIDTH), BF16)
    sem = ("parallel", "parallel", "arbitrary")
    a_out = pl.pallas_call(
        functools.partial(_diff_prompt_body, lam_init=lam_init),
        grid=(batch, A_HEADS, nq),
        in_specs=specs(COL_AQ, COL_AK, COL_AV) + [pl.BlockSpec((4, HEAD_DIM), full), pl.BlockSpec((1, LANES), full)],
        out_specs=out_spec, out_shape=out_shape,
        compiler_params=_params(sem), name="diff_prompt",
    )(qkv_bf, qkv_bf, qkv_bf, meta_bf, meta_bf, diff_lambda, a_g.reshape(1, LANES))
    b_g2 = jnp.concatenate([b_g, b_g]).reshape(1, LANES)
    b_out = pl.pallas_call(
        _stick_prompt_body,
        grid=(batch, B_HEADS // 2, nq),
        in_specs=specs(COL_BQ, COL_BK, COL_BV) + [pl.BlockSpec((1, LANES), full)],
        out_specs=out_spec, out_shape=out_shape,
        compiler_params=_params(sem), name="stick_prompt",
    )(qkv_bf, qkv_bf, qkv_bf, meta_bf, meta_bf, b_g2)
    return a_out, b_out


def _last_block(tail_ref, new_ref):
    tail = tail_ref[...].astype(BF16)
    new = new_ref[...]
    pad = jnp.zeros((LANES - tail.shape[0] - new.shape[0], LANES), BF16)
    return jnp.concatenate([tail, new, pad], axis=0)


def _diff_sample_body(q_ref, kn_ref, vn_ref, kc_ref, vc_ref, kt_ref, vt_ref, lam_ref, g_ref, o_ref, *, lam_init):
    ts = q_ref.shape[0]
    h = pl.program_id(1)
    nt = kt_ref.shape[0] // A_HEADS
    qst = _stack_heads_t(q_ref[...])
    key = lax.broadcasted_iota(jnp.int32, (LANES, 2 * ts), 0)
    head_rows = lambda ref, first, n: ref[pl.ds(first * A_HEADS + h, n, stride=A_HEADS), :].astype(BF16)

    def last_block(tail_ref, new_ref):
        pad = jnp.zeros((LANES - nt - ts, LANES), BF16)
        return jnp.concatenate([head_rows(tail_ref, 0, nt), new_ref[...], pad], axis=0)

    state = _softmax_step(qst, last_block(kt_ref, kn_ref), last_block(vt_ref, vn_ref), key < nt + ts,
                          _softmax_init(2 * ts))
    blk = ATT_BLOCK
    block = lambda ref: (lambda j: head_rows(ref, j * blk, blk))
    state = _softmax_scan(qst, kc_ref.shape[0] // (A_HEADS * blk), block(kc_ref), block(vc_ref), None, state)
    _diff_finish(state, lam_ref, g_ref, o_ref, lam_init)


def _stick_sample_body(q_ref, kn_ref, vn_ref, kc_ref, vc_ref, kt_ref, vt_ref, g_ref, o_ref):
    ts = q_ref.shape[0]
    nt = kt_ref.shape[0]
    qst = _stack_heads_t(q_ref[...])
    blk = ATT_BLOCK
    tri = _tri(blk)
    key = lax.broadcasted_iota(jnp.int32, (LANES, 2 * ts), 0)
    query = lax.broadcasted_iota(jnp.int32, (LANES, 2 * ts), 1) % ts
    mask = (key < nt) | ((key < nt + ts) & (key - nt < query))
    state = _stick_step(qst, _last_block(kt_ref, kn_ref), _last_block(vt_ref, vn_ref), tri[:LANES, :LANES], mask,
                        _stick_init(2 * ts))
    nb = kc_ref.shape[1] // blk
    block = lambda ref, j: ref[:, pl.ds(pl.multiple_of(j * blk, blk), blk)].astype(BF16)
    _, state = _stick_scan(
        nb, lambda n, st: _stick_step(qst, block(kc_ref, nb - 1 - n), block(vc_ref, nb - 1 - n), tri, None, st,
                                      kv_transposed=True),
        state)
    _stick_finish(state, g_ref, o_ref)


def _sample_attention(qkv_bf, ca_k, ca_v, cbt_k, cbt_v, cb_tail_k, cb_tail_v, dec_batch, dec_seq,
                      diff_lambda, a_g, b_g, lam_init):
    n_past = cbt_k.shape[2]
    main = n_past - N_META
    tail_blk = main // N_META
    full = lambda b, h: (0, 0)

    def new_specs(cq, ck, cv):
        return [pl.BlockSpec((dec_seq, LANES), lambda b, h: (b, cq + h)),
                pl.BlockSpec((dec_seq, LANES), lambda b, h: (b, ck + h)),
                pl.BlockSpec((dec_seq, LANES), lambda b, h: (b, cv + h))]

    b_main = pl.BlockSpec((None, LANES, main), lambda b, h: (b, h, 0))
    b_tail = pl.BlockSpec((None, N_META, LANES), lambda b, h: (b, 0, h))

    a_main = pl.BlockSpec((None, main * A_HEADS, LANES), lambda b, h: (b, 0, 0))
    a_tail = pl.BlockSpec((None, N_META * A_HEADS, LANES), lambda b, h: (b, tail_blk, 0))
    out_spec = pl.BlockSpec((dec_seq, LANES), lambda b, h: (b, h))
    out_shape = jax.ShapeDtypeStruct((dec_batch * dec_seq, A_WIDTH), BF16)
    sem = ("parallel", "parallel")
    a_out = pl.pallas_call(
        functools.partial(_diff_sample_body, lam_init=lam_init),
        grid=(dec_batch, A_HEADS),
        in_specs=(new_specs(COL_AQ, COL_AK, COL_AV) + [a_main, a_main, a_tail, a_tail]
                  + [pl.BlockSpec((4, HEAD_DIM), full), pl.BlockSpec((1, LANES), full)]),
        out_specs=out_spec, out_shape=out_shape,
        compiler_params=_params(sem), name="diff_sample",
    )(qkv_bf, qkv_bf, qkv_bf, ca_k, ca_v, ca_k, ca_v, diff_lambda, a_g.reshape(1, LANES))
    b_g2 = jnp.concatenate([b_g, b_g]).reshape(1, LANES)
    b_out = pl.pallas_call(
        _stick_sample_body,
        grid=(dec_batch, B_HEADS // 2),
        in_specs=(new_specs(COL_BQ, COL_BK, COL_BV) + [b_main, b_main, b_tail, b_tail]
                  + [pl.BlockSpec((1, LANES), full)]),
        out_specs=out_spec, out_shape=out_shape,
        compiler_params=_params(sem), name="stick_sample",
    )(qkv_bf, qkv_bf, qkv_bf, cbt_k, cbt_v, cb_tail_k, cb_tail_v, b_g2)
    return a_out, b_out


def _post_body(x_ref, a_ref, b_ref, wo_ref, g_ref, wqt_ref, sk_ref, h_ref, hn_ref, s_ref):
    h = (x_ref[...] + _dot(a_ref[...], wo_ref[:A_WIDTH, :]) + _dot(b_ref[...], wo_ref[A_WIDTH:, :]))
    h_ref[...] = h
    hn = h * lax.rsqrt(jnp.mean(h * h, axis=-1, keepdims=True) + EPS) * g_ref[...]
    hn_ref[...] = hn.astype(BF16)
    hnt = hn.T.astype(BF16)
    half = D_KEY // 2
    for hd in range(PEER_HEADS):
        qt = _dot(wqt_ref[hd * D_KEY:(hd + 1) * D_KEY, :], hnt).astype(BF16)
        for p in range(2):
            s_ref[2 * hd + p] = _dot(sk_ref[p], qt[p * half:(p + 1) * half, :])


def _post_attention(x2d, a_out, b_out, wo_bf, g_ffn, wq_bf, sk_bf, tm):
    n = x2d.shape[0]
    row = lambda i: (i, 0)
    full = lambda i: (0, 0)
    return pl.pallas_call(
        _post_body,
        grid=(n // tm,),
        in_specs=[pl.BlockSpec((tm, D_MODEL), row), pl.BlockSpec((tm, A_WIDTH), row), pl.BlockSpec((tm, B_WIDTH), row),
                  pl.BlockSpec((MIX_WIDTH, D_MODEL), full), pl.BlockSpec((1, D_MODEL), full),
                  pl.BlockSpec((PEER_HEADS * D_KEY, D_MODEL), full),
                  pl.BlockSpec((2, N_KEYS, D_KEY // 2), lambda i: (0, 0, 0))],
        out_specs=[pl.BlockSpec((tm, D_MODEL), row), pl.BlockSpec((tm, D_MODEL), row),
                   pl.BlockSpec((2 * PEER_HEADS, N_KEYS, tm), lambda i: (0, 0, i))],
        out_shape=[jax.ShapeDtypeStruct((n, D_MODEL), F32), jax.ShapeDtypeStruct((n, D_MODEL), BF16),
                   jax.ShapeDtypeStruct((2 * PEER_HEADS, N_KEYS, n), F32)],
        compiler_params=_params(("parallel",)),
        name="post",
    )(x2d, a_out, b_out, wo_bf, g_ffn.reshape(1, D_MODEL), wq_bf, sk_bf)


def _extract_top(s, iota, take, val_ref, aux, aux_ref):
    n_rows = float(s.shape[0])
    for r in range(take):
        m = jnp.max(s, axis=0, keepdims=True)
        pos = jnp.min(jnp.where(s == m, iota, n_rows), axis=0, keepdims=True)
        sel = iota == pos
        val_ref[r:r + 1, :] = m
        if aux is None:
            aux_ref[r:r + 1, :] = pos
        else:
            aux_ref[r:r + 1, :] = jnp.sum(jnp.where(sel, aux, 0.0), axis=0, keepdims=True)
        s = jnp.where(sel, -jnp.inf, s)


def _topk_body(s_ref, row_ref, col_ref, gate_ref, v0_ref, i0_ref, v1_ref, i1_ref, bs_ref, id_ref,
               rows_sc, cols_sc, gates_sc):
    tt = s_ref.shape[2]
    k = PEER_TOPK
    key_iota = lax.broadcasted_iota(jnp.int32, (N_KEYS, tt), 0).astype(F32)
    n_cand = k + 8 * 7 + 8
    cand_iota = lax.broadcasted_iota(jnp.int32, (n_cand, tt), 0).astype(F32)

    def head(h, carry):
        _extract_top(s_ref[2 * h], key_iota, k, v0_ref, None, i0_ref)
        _extract_top(s_ref[2 * h + 1], key_iota, k, v1_ref, None, i1_ref)
        v1_all, i1_all = v1_ref[...], i1_ref[...]
        cs = [v0_ref[0:1, :] + v1_all]
        ci = [i0_ref[0:1, :] * N_KEYS + i1_all]
        for r in range(1, 8):
            cs.append(v0_ref[r:r + 1, :] + v1_all[:8])
            ci.append(i0_ref[r:r + 1, :] * N_KEYS + i1_all[:8])
        cs.append(v0_ref[8:16, :] + v1_all[0:1])
        ci.append(i0_ref[8:16, :] * N_KEYS + i1_all[0:1])
        _extract_top(jnp.concatenate(cs, axis=0), cand_iota, k, bs_ref, jnp.concatenate(ci, axis=0), id_ref)
        best = bs_ref[...]
        e = jnp.exp(best - best[0:1])
        ids = id_ref[...]
        rows = jnp.floor(ids * (1.0 / N_KEYS))
        dst = pl.ds(pl.multiple_of(h * k, k), k)
        gates_sc[dst, :] = e / jnp.sum(e, axis=0, keepdims=True)
        rows_sc[dst, :] = rows
        cols_sc[dst, :] = ids - rows * N_KEYS
        return carry

    lax.fori_loop(0, PEER_HEADS, head, 0, unroll=4)
    row_ref[...] = rows_sc[...].T
    col_ref[...] = cols_sc[...].T
    gate_ref[...] = gates_sc[...].T


def _peer_topk(scores):
    n = scores.shape[2]
    tt = TOPK_TOKENS
    slots = PEER_HEADS * PEER_TOPK
    out = jax.ShapeDtypeStruct((n, slots), F32)
    return pl.pallas_call(
        _topk_body,
        grid=(n // tt,),
        in_specs=[pl.BlockSpec((2 * PEER_HEADS, N_KEYS, tt), lambda i: (0, 0, i))],
        out_specs=[pl.BlockSpec((tt, slots), lambda i: (i, 0))] * 3,
        out_shape=[out] * 3,
        scratch_shapes=[pltpu.VMEM((PEER_TOPK, tt), F32)] * 6 + [pltpu.VMEM((slots, tt), F32)] * 3,
        compiler_params=_params(("parallel",)),
        name="topk",
    )(scores)


def _gelu_x2(x):
    return x * (1.0 + lax.erf(x * (2.0 ** -0.5)))


def _peer_body(hn_ref, row_ref, col_ref, gate_ref, ut_ref, v_ref, h_ref, gf_ref, y_ref, w_sc, acc_sc):
    tb = hn_ref.shape[0]
    c = pl.program_id(1)
    per_chunk = v_ref.shape[0] // N_KEYS

    @pl.when(c == 0)
    def _():
        acc_sc[...] = jnp.zeros_like(acc_sc)
        sub = lax.broadcasted_iota(jnp.int32, (N_KEYS, LANES), 0).astype(F32)

        def build(t, carry):
            r = row_ref[pl.ds(t, 1), :]
            cc = col_ref[pl.ds(t, 1), :]
            g = gate_ref[pl.ds(t, 1), :] * 0.5
            at = jnp.where(sub == r, g, 0.0).astype(BF16)
            bt = jnp.where(sub == cc, 1.0, 0.0).astype(BF16)
            w_sc[pl.ds(pl.multiple_of(t * W_PITCH, 8), N_KEYS), :] = _dot_nt(at, bt)
            return carry

        lax.fori_loop(0, tb, build, 0, unroll=PEER_BUILD_UNROLL)

    x = hn_ref[...]
    total = None
    for s in range(v_ref.shape[0] // PEER_SUB):
        experts = slice(s * PEER_SUB, (s + 1) * PEER_SUB)
        pre = _dot(x, ut_ref[:, experts])
        first = c * per_chunk + s * (PEER_SUB // N_KEYS)
        w = jnp.concatenate([w_sc[pl.ds(first + a, tb, stride=W_PITCH), :] for a in range(PEER_SUB // N_KEYS)],
                            axis=1)
        part = _dot((_gelu_x2(pre) * w).astype(BF16), v_ref[experts, :])
        total = part if total is None else total + part
    acc_sc[...] += total

    @pl.when(c == pl.num_programs(1) - 1)
    def _():
        h = h_ref[...] + acc_sc[...]
        y_ref[...] = h * lax.rsqrt(jnp.mean(h * h, axis=-1, keepdims=True) + EPS) * gf_ref[...]


def _peer_dense(hn, rows, cols, gates, ut_bf, v_bf, h, g_final):
    n = hn.shape[0]
    tb = PEER_TOKENS
    slots = PEER_HEADS * PEER_TOPK
    tok = lambda i, c: (i, 0)
    chunk = lambda i, c: (c, 0)
    return pl.pallas_call(
        _peer_body,
        grid=(n // tb, N_EXPERTS // PEER_CHUNK),
        in_specs=[pl.BlockSpec((tb, D_MODEL), tok), pl.BlockSpec((tb, slots), tok), pl.BlockSpec((tb, slots), tok),
                  pl.BlockSpec((tb, slots), tok), pl.BlockSpec((D_MODEL, PEER_CHUNK), lambda i, c: (0, c)),
                  pl.BlockSpec((PEER_CHUNK, D_MODEL), chunk), pl.BlockSpec((tb, D_MODEL), tok),
                  pl.BlockSpec((1, D_MODEL), lambda i, c: (0, 0))],
        out_specs=pl.BlockSpec((tb, D_MODEL), tok),
        out_shape=jax.ShapeDtypeStruct((n, D_MODEL), F32),
        scratch_shapes=[pltpu.VMEM((tb * W_PITCH, LANES), F32), pltpu.VMEM((tb, D_MODEL), F32)],
        compiler_params=_params(("parallel", "arbitrary")),
        name="peer",
    )(hn, rows, cols, gates, ut_bf, v_bf, h, g_final.reshape(1, D_MODEL))


def _finish(x2d, a_out, b_out, wo_bf, g_ffn, wq_bf, sk_bf, u_bf, v_bf, g_final):
    h, hn, scores = _post_attention(x2d, a_out, b_out, wo_bf, g_ffn, wq_bf, sk_bf, min(POST_TILE, x2d.shape[0]))
    rows, cols, gates = _peer_topk(scores)
    return _peer_dense(hn, rows, cols, gates, u_bf, v_bf, h, g_final)


def kernel(x_prompt, x_sample, cache_a_k, cache_a_v, cache_b_k, cache_b_v, meta_tokens, g_attn, w_qkv,
           diff_lambda, a_norm_g, b_norm_g, w_o, g_ffn, w_peer_q, peer_sub_keys, peer_u, peer_v, g_final):
    batch, seq, _ = x_prompt.shape
    dec_batch, dec_seq, _ = x_sample.shape
    depth = w_qkv.shape[0]
    assert depth == 1, "single-layer step"
    n_past = cache_a_k.shape[2]
    lam_init = 0.8 - 0.6 * math.exp(-0.3 * 0)

    w_bf = w_qkv[0].astype(BF16)
    wo_bf = w_o[0].astype(BF16)
    wq_bf = w_peer_q[0].astype(BF16).T
    sk_bf = peer_sub_keys[0].astype(BF16)
    u_bf = peer_u[0].astype(BF16).T
    v_bf = peer_v[0].astype(BF16)
    g1, g2 = g_attn[0], g_ffn[0]
    lam_p, a_g, b_g = diff_lambda[0], a_norm_g[0], b_norm_g[0]

    xp = x_prompt.reshape(batch * seq, D_MODEL)
    xs = x_sample.reshape(dec_batch * dec_seq, D_MODEL)
    p_ak, p_av, p_bk, p_bv, p_bf = _qkv_project(xp, N_META + jnp.arange(seq, dtype=jnp.int32), g1, w_bf, ROW_TILE)
    m_ak, m_av, m_bk, m_bv, m_bf = _qkv_project(meta_tokens.astype(F32), jnp.arange(N_META, dtype=jnp.int32),
                                                g1, w_bf, N_META)
    s_ak, s_av, s_bk, s_bv, s_bf = _qkv_project(xs, n_past + jnp.arange(dec_seq, dtype=jnp.int32), g1, w_bf, dec_seq)

    meta_pad = jnp.pad(m_bf, ((0, LANES - N_META), (0, 0)))
    pa, pb = _prompt_attention(p_bf, meta_pad, batch, seq, lam_p, a_g, b_g, lam_init)
    y_prompt = _finish(xp, pa, pb, wo_bf, g2, wq_bf, sk_bf, u_bf, v_bf, g_final)

    ca_k = cache_a_k[0].reshape(dec_batch, n_past * A_HEADS, 2 * HEAD_DIM)
    ca_v = cache_a_v[0].reshape(dec_batch, n_past * A_HEADS, 2 * HEAD_DIM)
    feature_major = lambda c: jnp.transpose(c[0], (0, 2, 3, 1)).reshape(dec_batch, B_WIDTH, n_past)
    tail_rows = lambda c: c[0, :, n_past - N_META:].reshape(dec_batch, N_META, B_WIDTH)
    sa, sb = _sample_attention(s_bf, ca_k, ca_v, feature_major(cache_b_k), feature_major(cache_b_v),
                               tail_rows(cache_b_k), tail_rows(cache_b_v), dec_batch, dec_seq,
                               lam_p, a_g, b_g, lam_init)
    y_sample = _finish(xs, sa, sb, wo_bf, g2, wq_bf, sk_bf, u_bf, v_bf, g_final)

    def prompt_cache(meta_rows, frame_rows, heads):
        m = jnp.broadcast_to(meta_rows[None], (batch, N_META, A_WIDTH))
        full = jnp.concatenate([m, frame_rows.reshape(batch, seq, A_WIDTH)], axis=1)
        return full.reshape(1, batch, N_META + seq, heads, A_WIDTH // heads)

    def sample_cache(rows, heads):
        return rows.reshape(1, dec_batch, dec_seq, heads, A_WIDTH // heads)

    return (y_prompt.reshape(batch, seq, D_MODEL), y_sample.reshape(dec_batch, dec_seq, D_MODEL),
            prompt_cache(m_ak, p_ak, A_HEADS), prompt_cache(m_av, p_av, A_HEADS),
            prompt_cache(m_bk, p_bk, B_HEADS), prompt_cache(m_bv, p_bv, B_HEADS),
            sample_cache(s_ak, A_HEADS), sample_cache(s_av, A_HEADS),
            sample_cache(s_bk, B_HEADS), sample_cache(s_bv, B_HEADS))
```

```python
import functools
import math

import jax
import jax.numpy as jnp
from jax import lax
from jax.experimental import pallas as pl
from jax.experimental.pallas import tpu as pltpu

F32 = jnp.float32
BF16 = jnp.bfloat16

D_MODEL = 1024
CHUNK = 64
N_META = 16
HEAD_DIM = 64
A_HEADS = 4
B_HEADS = 8
A_WIDTH = A_HEADS * 2 * HEAD_DIM
B_WIDTH = B_HEADS * HEAD_DIM
MIX_WIDTH = A_WIDTH + B_WIDTH
QKV_WIDTH = 3 * MIX_WIDTH
ROT_DIM = HEAD_DIM // 4
ROPE_THETA = 500000.0
N_KEYS = 128
N_EXPERTS = N_KEYS * N_KEYS
PEER_HEADS = 8
PEER_TOPK = 16
D_KEY = 256
EPS = 1e-6
NEG_BIG = -1e30

LANES = 128
ROW_TILE = 256
POST_TILE = 512
ATT_BLOCK = 256
TOPK_TOKENS = 128
PEER_TOKENS = 256
PEER_CHUNK = 2048
PEER_SUB = 512
PEER_BUILD_UNROLL = 32
W_PITCH = N_KEYS + 8
STICK_PIECE = 512
STICK_EXIT = -110.0
VMEM_LIMIT = 56 * 1024 * 1024

COL_AQ, COL_AK, COL_AV = 0, 4, 8
COL_BQ, COL_BK, COL_BV = 12, 16, 20


def _dot(a, b):
    return jnp.dot(a, b, preferred_element_type=F32)


def _dot_nt(a, b):
    return lax.dot_general(a, b, (((1,), (1,)), ((), ())), preferred_element_type=F32)


def _dot_tn(a, b):
    return lax.dot_general(a, b, (((0,), (0,)), ((), ())), preferred_element_type=F32)


def _params(sem):
    return pltpu.CompilerParams(dimension_semantics=sem, vmem_limit_bytes=VMEM_LIMIT)


def _qkv_body(x_ref, g_ref, w_ref, c_ref, s1_ref, s2_ref, ak_ref, av_ref, bk_ref, bv_ref, bf_ref):
    x = x_ref[...]
    xn = (x * lax.rsqrt(jnp.mean(x * x, axis=-1, keepdims=True) + EPS) * g_ref[...]).astype(BF16)
    cos_t = c_ref[...]
    sin_lo = s1_ref[...]
    sin_hi = s2_ref[...]
    f32_outs = {1: ak_ref, 2: av_ref, 4: bk_ref, 5: bv_ref}
    for grp in range(6):
        for half in range(2):
            y2 = _dot(xn, w_ref[:, (grp * 4 + half * 2) * LANES:(grp * 4 + half * 2 + 2) * LANES])
            for k in range(2):
                j = half * 2 + k
                cb = grp * 4 + j
                y = y2[:, k * LANES:(k + 1) * LANES]
                if grp in (0, 1):
                    y = (y * cos_t + pltpu.roll(y, LANES - ROT_DIM // 2, 1) * sin_lo
                         + pltpu.roll(y, ROT_DIM // 2, 1) * sin_hi)
                if grp in f32_outs:
                    f32_outs[grp][:, j * LANES:(j + 1) * LANES] = y
                if grp in (0, 3):
                    y = y * (HEAD_DIM ** -0.5)
                bf_ref[:, cb * LANES:(cb + 1) * LANES] = y.astype(BF16)


def _rope_tables(pos):
    half = ROT_DIM // 2
    inv_freq = ROPE_THETA ** (-jnp.arange(half, dtype=F32) * 2.0 / ROT_DIM)
    ang = pos.astype(F32)[:, None] * inv_freq[None, :]
    cos, sin = jnp.cos(ang), jnp.sin(ang)
    t = pos.shape[0]
    pad = jnp.zeros((t, HEAD_DIM - ROT_DIM), F32)
    zero = jnp.zeros((t, half), F32)
    cos_t = jnp.concatenate([cos, cos, pad + 1.0], axis=1)
    sin_lo = jnp.concatenate([-sin, zero, pad], axis=1)
    sin_hi = jnp.concatenate([zero, sin, pad], axis=1)
    tile = lambda a: jnp.concatenate([a, a], axis=1)
    return tile(cos_t), tile(sin_lo), tile(sin_hi)


def _qkv_project(x2d, pos, g, w_bf, tm):
    n = x2d.shape[0]
    t = pos.shape[0]
    per = t // tm
    tabs = _rope_tables(pos)
    row = lambda i: (i, 0)
    tab = lambda i: (i % per, 0)
    full = lambda i: (0, 0)
    f32_out = jax.ShapeDtypeStruct((n, A_WIDTH), F32)
    return pl.pallas_call(
        _qkv_body,
        grid=(n // tm,),
        in_specs=[pl.BlockSpec((tm, D_MODEL), row), pl.BlockSpec((1, D_MODEL), full),
                  pl.BlockSpec((D_MODEL, QKV_WIDTH), full),
                  pl.BlockSpec((tm, LANES), tab), pl.BlockSpec((tm, LANES), tab), pl.BlockSpec((tm, LANES), tab)],
        out_specs=[pl.BlockSpec((tm, A_WIDTH), row)] * 4 + [pl.BlockSpec((tm, QKV_WIDTH), row)],
        out_shape=[f32_out] * 4 + [jax.ShapeDtypeStruct((n, QKV_WIDTH), BF16)],
        compiler_params=_params(("parallel",)),
        name="qkv",
    )(x2d, g.reshape(1, D_MODEL), w_bf, *tabs)


def _stack_heads_t(q):
    qt = q.astype(F32).T
    sub = lax.broadcasted_iota(jnp.int32, qt.shape, 0)
    first = jnp.where(sub < HEAD_DIM, qt, 0.0)
    second = jnp.where(sub >= HEAD_DIM, qt, 0.0)
    return jnp.concatenate([first, second], axis=1).astype(BF16)


def _softmax_init(n2):
    return jnp.full((1, n2), NEG_BIG, F32), jnp.zeros((1, n2), F32)


def _softmax_update(s, m, l):
    m_new = jnp.maximum(m, jnp.max(s, axis=0, keepdims=True))
    alpha = jnp.exp(m - m_new)
    p = jnp.exp(s - m_new)
    return m_new, alpha, alpha * l + jnp.sum(p, axis=0, keepdims=True), p.astype(BF16)


def _softmax_first(kb, vb, mask, scratch):
    qst_sc, _, _, acc_sc = scratch
    m, l = _softmax_init(qst_sc.shape[1])
    sc = jnp.where(mask, _dot(kb, qst_sc[...]), NEG_BIG)
    m, _, l, p = _softmax_update(sc, m, l)
    acc_sc[...] = _dot_tn(vb, p)
    return m, l


def _softmax_scratch(tk, n2):
    return [pltpu.VMEM((LANES, n2), BF16), pltpu.VMEM((tk, n2), F32), pltpu.VMEM((tk, n2), BF16),
            pltpu.VMEM((LANES, n2), F32)]


def _softmax_scan(n_blocks, kb_fn, vb_fn, last_mask, state, scratch):
    qst_sc, s_sc, p_sc, acc_sc = scratch
    n2 = qst_sc.shape[1]
    mm_cols = min(2 * LANES, n2)

    def block_step(m, l, v_prev, k_next, mask):
        m_out, l_out = [], []
        for c0 in range(0, n2, mm_cols):
            wide = slice(c0, c0 + mm_cols)
            pv = _dot_tn(v_prev, p_sc[:, wide])
            tiles = []
            for t0 in range(c0, c0 + mm_cols, LANES):
                cols = slice(t0, t0 + LANES)
                sc = s_sc[:, cols]
                if mask is not None:
                    sc = jnp.where(mask[:, cols], sc, NEG_BIG)
                tiles.append((cols, sc))
            if k_next is not None:
                s_sc[:, wide] = _dot(k_next, qst_sc[:, wide])
            for cols, sc in tiles:
                m_new, alpha, l_new, p = _softmax_update(sc, m[:, cols], l[:, cols])
                p_sc[:, cols] = p
                off = cols.start - c0
                acc_sc[:, cols] = alpha * (acc_sc[:, cols] + pv[:, off:off + LANES])
                m_out.append(m_new)
                l_out.append(l_new)
        return jnp.concatenate(m_out, axis=1), jnp.concatenate(l_out, axis=1)

    m, l = state
    s_sc[...] = _dot(kb_fn(0), qst_sc[...])
    p_sc[...] = jnp.zeros(p_sc.shape, BF16)
    last = n_blocks - 1

    def body(j, carry):
        return block_step(*carry, vb_fn(jnp.maximum(j - 1, 0)), kb_fn(j + 1), None)

    m, l = lax.fori_loop(0, last // 2, lambda jj, carry: body(2 * jj + 1, body(2 * jj, carry)), (m, l))
    m, l = lax.cond(last % 2 == 1, lambda carry: body(last - 1, carry), lambda carry: carry, (m, l))
    m, l = block_step(m, l, vb_fn(jnp.maximum(last - 1, 0)), None, last_mask)
    acc_sc[...] += _dot_tn(vb_fn(last), p_sc[...])
    return m, l


def _diff_finish(l, acc_sc, lam_ref, g_ref, o_ref, lam_init):
    tq = o_ref.shape[0]
    acc = acc_sc[...]
    lp = lam_ref[...]
    lam = (jnp.exp(jnp.sum(lp[0:1] * lp[1:2], axis=1, keepdims=True))
           - jnp.exp(jnp.sum(lp[2:3] * lp[3:4], axis=1, keepdims=True)) + lam_init)
    on = (acc / l).T
    o = on[:tq] - lam * on[tq:]
    o = o * lax.rsqrt(jnp.mean(o * o, axis=1, keepdims=True) + EPS) * g_ref[...] * (1.0 - lam_init)
    o_ref[...] = o.astype(o_ref.dtype)


def _stick_pieces(n2):
    width = min(STICK_PIECE, n2)
    return [slice(c0, c0 + width) for c0 in range(0, n2, width)]


def _stick_init(n2):
    return tuple((jnp.zeros((1, p.stop - p.start), F32), jnp.zeros((LANES, p.stop - p.start), F32))
                 for p in _stick_pieces(n2))


def _stick_step(qst, kb, vb, tri, mask_fn, state, kv_transposed=False):
    out = []
    for cols, (c, acc) in zip(_stick_pieces(qst.shape[1]), state):
        q = qst[:, cols]
        z = _dot_tn(kb, q) if kv_transposed else _dot(kb, q)
        lm = -(jnp.maximum(z, 0.0) + jnp.log1p(jnp.exp(-jnp.abs(z))))
        if mask_fn is not None:
            mask = mask_fn(lax.broadcasted_iota(jnp.int32, z.shape, 0),
                           lax.broadcasted_iota(jnp.int32, z.shape, 1) + cols.start)
            lm = jnp.where(mask, lm, 0.0)
        hi = lm.astype(BF16)
        lo = (lm - hi.astype(F32)).astype(BF16)
        cum = _dot(tri, hi) + _dot(tri, lo)
        a = jnp.exp(z + cum + c)
        if mask_fn is not None:
            a = jnp.where(mask, a, 0.0)
        a = a.astype(BF16)
        out.append((c + cum[0:1, :], acc + (_dot(vb, a) if kv_transposed else _dot_tn(vb, a))))
    return tuple(out)


def _stick_cmax(state):
    return functools.reduce(jnp.maximum, [jnp.max(c) for c, _ in state])


def _tri(n):
    r = lax.broadcasted_iota(jnp.int32, (n, n), 0)
    c = lax.broadcasted_iota(jnp.int32, (n, n), 1)
    return jnp.where(c >= r, 1.0, 0.0).astype(BF16)


def _stick_finish(state, g_ref, o_ref):
    tq = o_ref.shape[0]
    acc = jnp.concatenate([a.T for _, a in state], axis=0)
    lane = lax.broadcasted_iota(jnp.int32, (tq, LANES), 1)
    first = lane < HEAD_DIM
    o = jnp.where(first, acc[:tq], acc[tq:])
    sq = o * o
    ss0 = jnp.sum(jnp.where(first, sq, 0.0), axis=1, keepdims=True)
    ss1 = jnp.sum(jnp.where(first, 0.0, sq), axis=1, keepdims=True)
    ms = jnp.where(first, ss0, ss1) * (1.0 / HEAD_DIM)
    o_ref[...] = (o * lax.rsqrt(ms + EPS) * g_ref[...]).astype(o_ref.dtype)


def _stick_scan(n_blocks, step_fn, state):
    def cond(st):
        return jnp.logical_and(st[0] < n_blocks, st[1] > STICK_EXIT)

    def body(st):
        state = step_fn(st[0], st[2])
        return st[0] + 1, _stick_cmax(state), state

    out = lax.while_loop(cond, body, (jnp.int32(0), _stick_cmax(state), state))
    return out[1], out[2]


def _diff_prompt_body(q_ref, k_ref, v_ref, km_ref, vm_ref, lam_ref, g_ref, o_ref, *scratch, lam_init):
    tq = q_ref.shape[0]
    i = pl.program_id(2)
    scratch[0][...] = _stack_heads_t(q_ref[...])
    nm = km_ref.shape[0]
    key = lax.broadcasted_iota(jnp.int32, (nm, 2 * tq), 0)
    state = _softmax_first(km_ref[...], vm_ref[...], key < N_META, scratch)
    key = lax.broadcasted_iota(jnp.int32, (tq, 2 * tq), 0)
    query = lax.broadcasted_iota(jnp.int32, (tq, 2 * tq), 1) % tq
    block = lambda ref: (lambda j: ref[pl.ds(pl.multiple_of(j * tq, tq), tq), :])
    _, l = _softmax_scan(i + 1, block(k_ref), block(v_ref), (key // CHUNK) <= (query // CHUNK), state, scratch)
    _diff_finish(l, scratch[3], lam_ref, g_ref, o_ref, lam_init)


def _stick_prompt_body(q_ref, k_ref, v_ref, km_ref, vm_ref, g_ref, o_ref):
    tq = q_ref.shape[0]
    i = pl.program_id(2)
    qst = _stack_heads_t(q_ref[...])
    tri = _tri(tq)
    block = lambda ref, j: ref[pl.ds(pl.multiple_of(j * tq, tq), tq), :]
    state = _stick_step(qst, block(k_ref, i), block(v_ref, i), tri, lambda key, col: key < col % tq,
                        _stick_init(2 * tq))
    cmax, state = _stick_scan(
        i, lambda n, st: _stick_step(qst, block(k_ref, i - 1 - n), block(v_ref, i - 1 - n), tri, None, st), state)
    nm = km_ref.shape[0]
    state = lax.cond(cmax > STICK_EXIT,
                     lambda st: _stick_step(qst, km_ref[...], vm_ref[...], tri[:nm, :nm],
                                            lambda key, col: key < N_META, st),
                     lambda st: st, state)
    _stick_finish(state, g_ref, o_ref)


def _prompt_attention(qkv_bf, meta_bf, batch, seq, diff_lambda, a_g, b_g, lam_init):
    tq = ATT_BLOCK
    nq = seq // tq
    n = batch * seq
    qmap = lambda off: (lambda b, h, i: (b * nq + i, off + h))
    kvmap = lambda off: (lambda b, h, i: (b, off + h))
    mmap = lambda off: (lambda b, h, i: (0, off + h))
    full = lambda b, h, i: (0, 0)
    nm = meta_bf.shape[0]

    def specs(cq, ck, cv):
        return [pl.BlockSpec((tq, LANES), qmap(cq)),
                pl.BlockSpec((seq, LANES), kvmap(ck)), pl.BlockSpec((seq, LANES), kvmap(cv)),
                pl.BlockSpec((nm, LANES), mmap(ck)), pl.BlockSpec((nm, LANES), mmap(cv))]

    out_spec = pl.BlockSpec((tq, LANES), lambda b, h, i: (b * nq + i, h))
    out_shape = jax.ShapeDtypeStruct((n, A_WIDTH), BF16)
    sem = ("parallel", "parallel", "arbitrary")
    a_out = pl.pallas_call(
        functools.partial(_diff_prompt_body, lam_init=lam_init),
        grid=(batch, A_HEADS, nq),
        in_specs=specs(COL_AQ, COL_AK, COL_AV) + [pl.BlockSpec((4, HEAD_DIM), full), pl.BlockSpec((1, LANES), full)],
        out_specs=out_spec, out_shape=out_shape, scratch_shapes=_softmax_scratch(tq, 2 * tq),
        compiler_params=_params(sem), name="diff_prompt",
    )(qkv_bf, qkv_bf, qkv_bf, meta_bf, meta_bf, diff_lambda, a_g.reshape(1, LANES))
    b_g2 = jnp.concatenate([b_g, b_g]).reshape(1, LANES)
    b_out = pl.pallas_call(
        _stick_prompt_body,
        grid=(batch, B_HEADS // 2, nq),
        in_specs=specs(COL_BQ, COL_BK, COL_BV) + [pl.BlockSpec((1, LANES), full)],
        out_specs=out_spec, out_shape=out_shape,
        compiler_params=_params(sem), name="stick_prompt",
    )(qkv_bf, qkv_bf, qkv_bf, meta_bf, meta_bf, b_g2)
    return a_out, b_out


def _last_block(tail_ref, new_ref):
    tail = tail_ref[...].astype(BF16)
    new = new_ref[...]
    pad = jnp.zeros((LANES - tail.shape[0] - new.shape[0], LANES), BF16)
    return jnp.concatenate([tail, new, pad], axis=0)


def _diff_sample_body(q_ref, kn_ref, vn_ref, kc_ref, vc_ref, kt_ref, vt_ref, lam_ref, g_ref, o_ref, *scratch,
                      lam_init):
    ts = q_ref.shape[0]
    h = pl.program_id(1)
    nt = kt_ref.shape[0] // A_HEADS
    scratch[0][...] = _stack_heads_t(q_ref[...])
    key = lax.broadcasted_iota(jnp.int32, (LANES, 2 * ts), 0)
    head_rows = lambda ref, first, n: ref[pl.ds(first * A_HEADS + h, n, stride=A_HEADS), :].astype(BF16)

    def last_block(tail_ref, new_ref):
        pad = jnp.zeros((LANES - nt - ts, LANES), BF16)
        return jnp.concatenate([head_rows(tail_ref, 0, nt), new_ref[...], pad], axis=0)

    state = _softmax_first(last_block(kt_ref, kn_ref), last_block(vt_ref, vn_ref), key < nt + ts, scratch)
    blk = ATT_BLOCK
    block = lambda ref: (lambda j: head_rows(ref, j * blk, blk))
    _, l = _softmax_scan(kc_ref.shape[0] // (A_HEADS * blk), block(kc_ref), block(vc_ref), None, state, scratch)
    _diff_finish(l, scratch[3], lam_ref, g_ref, o_ref, lam_init)


def _stick_sample_body(q_ref, kn_ref, vn_ref, kc_ref, vc_ref, kt_ref, vt_ref, g_ref, o_ref):
    ts = q_ref.shape[0]
    nt = kt_ref.shape[0]
    qst = _stack_heads_t(q_ref[...])
    blk = ATT_BLOCK
    tri = _tri(blk)
    last_mask = lambda key, col: (key < nt) | ((key < nt + ts) & (key - nt < col % ts))
    state = _stick_step(qst, _last_block(kt_ref, kn_ref), _last_block(vt_ref, vn_ref), tri[:LANES, :LANES],
                        last_mask, _stick_init(2 * ts))
    nb = kc_ref.shape[1] // blk
    block = lambda ref, j: ref[:, pl.ds(pl.multiple_of(j * blk, blk), blk)].astype(BF16)
    _, state = _stick_scan(
        nb, lambda n, st: _stick_step(qst, block(kc_ref, nb - 1 - n), block(vc_ref, nb - 1 - n), tri, None, st,
                                      kv_transposed=True),
        state)
    _stick_finish(state, g_ref, o_ref)


def _sample_attention(qkv_bf, ca_k, ca_v, cbt_k, cbt_v, cb_tail_k, cb_tail_v, dec_batch, dec_seq,
                      diff_lambda, a_g, b_g, lam_init):
    n_past = cbt_k.shape[2]
    main = n_past - N_META
    tail_blk = main // N_META
    full = lambda b, h: (0, 0)

    def new_specs(cq, ck, cv):
        return [pl.BlockSpec((dec_seq, LANES), lambda b, h: (b, cq + h)),
                pl.BlockSpec((dec_seq, LANES), lambda b, h: (b, ck + h)),
                pl.BlockSpec((dec_seq, LANES), lambda b, h: (b, cv + h))]

    b_main = pl.BlockSpec((None, LANES, main), lambda b, h: (b, h, 0))
    b_tail = pl.BlockSpec((None, N_META, LANES), lambda b, h: (b, 0, h))

    a_main = pl.BlockSpec((None, main * A_HEADS, LANES), lambda b, h: (b, 0, 0))
    a_tail = pl.BlockSpec((None, N_META * A_HEADS, LANES), lambda b, h: (b, tail_blk, 0))
    out_spec = pl.BlockSpec((dec_seq, LANES), lambda b, h: (b, h))
    out_shape = jax.ShapeDtypeStruct((dec_batch * dec_seq, A_WIDTH), BF16)
    sem = ("parallel", "parallel")
    a_out = pl.pallas_call(
        functools.partial(_diff_sample_body, lam_init=lam_init),
        grid=(dec_batch, A_HEADS),
        in_specs=(new_specs(COL_AQ, COL_AK, COL_AV) + [a_main, a_main, a_tail, a_tail]
                  + [pl.BlockSpec((4, HEAD_DIM), full), pl.BlockSpec((1, LANES), full)]),
        out_specs=out_spec, out_shape=out_shape, scratch_shapes=_softmax_scratch(ATT_BLOCK, 2 * dec_seq),
        compiler_params=_params(sem), name="diff_sample",
    )(qkv_bf, qkv_bf, qkv_bf, ca_k, ca_v, ca_k, ca_v, diff_lambda, a_g.reshape(1, LANES))
    b_g2 = jnp.concatenate([b_g, b_g]).reshape(1, LANES)
    b_out = pl.pallas_call(
        _stick_sample_body,
        grid=(dec_batch, B_HEADS // 2),
        in_specs=(new_specs(COL_BQ, COL_BK, COL_BV) + [b_main, b_main, b_tail, b_tail]
                  + [pl.BlockSpec((1, LANES), full)]),
        out_specs=out_spec, out_shape=out_shape,
        compiler_params=_params(sem), name="stick_sample",
    )(qkv_bf, qkv_bf, qkv_bf, cbt_k, cbt_v, cb_tail_k, cb_tail_v, b_g2)
    return a_out, b_out


def _post_body(x_ref, a_ref, b_ref, wo_ref, g_ref, wqt_ref, sk_ref, h_ref, hn_ref, s_ref):
    h = (x_ref[...] + _dot(a_ref[...], wo_ref[:A_WIDTH, :]) + _dot(b_ref[...], wo_ref[A_WIDTH:, :]))
    h_ref[...] = h
    hn = h * lax.rsqrt(jnp.mean(h * h, axis=-1, keepdims=True) + EPS) * g_ref[...]
    hn_ref[...] = hn.astype(BF16)
    hnt = hn.T.astype(BF16)
    half = D_KEY // 2
    for hd in range(PEER_HEADS):
        qt = _dot(wqt_ref[hd * D_KEY:(hd + 1) * D_KEY, :], hnt).astype(BF16)
        for p in range(2):
            s_ref[2 * hd + p] = _dot(sk_ref[p], qt[p * half:(p + 1) * half, :])


def _post_attention(x2d, a_out, b_out, wo_bf, g_ffn, wq_bf, sk_bf, tm):
    n = x2d.shape[0]
    row = lambda i: (i, 0)
    full = lambda i: (0, 0)
    return pl.pallas_call(
        _post_body,
        grid=(n // tm,),
        in_specs=[pl.BlockSpec((tm, D_MODEL), row), pl.BlockSpec((tm, A_WIDTH), row), pl.BlockSpec((tm, B_WIDTH), row),
                  pl.BlockSpec((MIX_WIDTH, D_MODEL), full), pl.BlockSpec((1, D_MODEL), full),
                  pl.BlockSpec((PEER_HEADS * D_KEY, D_MODEL), full),
                  pl.BlockSpec((2, N_KEYS, D_KEY // 2), lambda i: (0, 0, 0))],
        out_specs=[pl.BlockSpec((tm, D_MODEL), row), pl.BlockSpec((tm, D_MODEL), row),
                   pl.BlockSpec((2 * PEER_HEADS, N_KEYS, tm), lambda i: (0, 0, i))],
        out_shape=[jax.ShapeDtypeStruct((n, D_MODEL), F32), jax.ShapeDtypeStruct((n, D_MODEL), BF16),
                   jax.ShapeDtypeStruct((2 * PEER_HEADS, N_KEYS, n), F32)],
        compiler_params=_params(("parallel",)),
        name="post",
    )(x2d, a_out, b_out, wo_bf, g_ffn.reshape(1, D_MODEL), wq_bf, sk_bf)


def _extract_top(s, iota, take, val_ref, aux, aux_ref):
    n_rows = float(s.shape[0])
    for r in range(take):
        m = jnp.max(s, axis=0, keepdims=True)
        pos = jnp.min(jnp.where(s == m, iota, n_rows), axis=0, keepdims=True)
        sel = iota == pos
        val_ref[r:r + 1, :] = m
        if aux is None:
            aux_ref[r:r + 1, :] = pos
        else:
            aux_ref[r:r + 1, :] = jnp.sum(jnp.where(sel, aux, 0.0), axis=0, keepdims=True)
        s = jnp.where(sel, -jnp.inf, s)


def _topk_body(s_ref, row_ref, col_ref, gate_ref, v0_ref, i0_ref, v1_ref, i1_ref, bs_ref, id_ref,
               rows_sc, cols_sc, gates_sc):
    tt = s_ref.shape[2]
    k = PEER_TOPK
    key_iota = lax.broadcasted_iota(jnp.int32, (N_KEYS, tt), 0).astype(F32)
    n_cand = k + 8 * 7 + 8
    cand_iota = lax.broadcasted_iota(jnp.int32, (n_cand, tt), 0).astype(F32)

    def head(h, carry):
        _extract_top(s_ref[2 * h], key_iota, k, v0_ref, None, i0_ref)
        _extract_top(s_ref[2 * h + 1], key_iota, k, v1_ref, None, i1_ref)
        v1_all, i1_all = v1_ref[...], i1_ref[...]
        cs = [v0_ref[0:1, :] + v1_all]
        ci = [i0_ref[0:1, :] * N_KEYS + i1_all]
        for r in range(1, 8):
            cs.append(v0_ref[r:r + 1, :] + v1_all[:8])
            ci.append(i0_ref[r:r + 1, :] * N_KEYS + i1_all[:8])
        cs.append(v0_ref[8:16, :] + v1_all[0:1])
        ci.append(i0_ref[8:16, :] * N_KEYS + i1_all[0:1])
        _extract_top(jnp.concatenate(cs, axis=0), cand_iota, k, bs_ref, jnp.concatenate(ci, axis=0), id_ref)
        best = bs_ref[...]
        e = jnp.exp(best - best[0:1])
        ids = id_ref[...]
        rows = jnp.floor(ids * (1.0 / N_KEYS))
        dst = pl.ds(pl.multiple_of(h * k, k), k)
        gates_sc[dst, :] = e / jnp.sum(e, axis=0, keepdims=True)
        rows_sc[dst, :] = rows
        cols_sc[dst, :] = ids - rows * N_KEYS
        return carry

    lax.fori_loop(0, PEER_HEADS, head, 0, unroll=4)
    row_ref[...] = rows_sc[...].T
    col_ref[...] = cols_sc[...].T
    gate_ref[...] = gates_sc[...].T


def _peer_topk(scores):
    n = scores.shape[2]
    tt = TOPK_TOKENS
    slots = PEER_HEADS * PEER_TOPK
    out = jax.ShapeDtypeStruct((n, slots), F32)
    return pl.pallas_call(
        _topk_body,
        grid=(n // tt,),
        in_specs=[pl.BlockSpec((2 * PEER_HEADS, N_KEYS, tt), lambda i: (0, 0, i))],
        out_specs=[pl.BlockSpec((tt, slots), lambda i: (i, 0))] * 3,
        out_shape=[out] * 3,
        scratch_shapes=[pltpu.VMEM((PEER_TOPK, tt), F32)] * 6 + [pltpu.VMEM((slots, tt), F32)] * 3,
        compiler_params=_params(("parallel",)),
        name="topk",
    )(scores)


def _gelu_x2(x):
    return x * (1.0 + lax.erf(x * (2.0 ** -0.5)))


def _peer_body(hn_ref, row_ref, col_ref, gate_ref, ut_ref, v_ref, h_ref, gf_ref, y_ref, w_sc, acc_sc):
    tb = hn_ref.shape[0]
    c = pl.program_id(1)
    per_chunk = v_ref.shape[0] // N_KEYS

    @pl.when(c == 0)
    def _():
        acc_sc[...] = jnp.zeros_like(acc_sc)
        sub = lax.broadcasted_iota(jnp.int32, (N_KEYS, LANES), 0).astype(F32)

        def build(t, carry):
            r = row_ref[pl.ds(t, 1), :]
            cc = col_ref[pl.ds(t, 1), :]
            g = gate_ref[pl.ds(t, 1), :] * 0.5
            at = jnp.where(sub == r, g, 0.0).astype(BF16)
            bt = jnp.where(sub == cc, 1.0, 0.0).astype(BF16)
            w_sc[pl.ds(pl.multiple_of(t * W_PITCH, 8), N_KEYS), :] = _dot_nt(at, bt)
            return carry

        lax.fori_loop(0, tb, build, 0, unroll=PEER_BUILD_UNROLL)

    x = hn_ref[...]
    total = None
    for s in range(v_ref.shape[0] // PEER_SUB):
        experts = slice(s * PEER_SUB, (s + 1) * PEER_SUB)
        pre = _dot(x, ut_ref[:, experts])
        first = c * per_chunk + s * (PEER_SUB // N_KEYS)
        w = jnp.concatenate([w_sc[pl.ds(first + a, tb, stride=W_PITCH), :] for a in range(PEER_SUB // N_KEYS)],
                            axis=1)
        part = _dot((_gelu_x2(pre) * w).astype(BF16), v_ref[experts, :])
        total = part if total is None else total + part
    acc_sc[...] += total

    @pl.when(c == pl.num_programs(1) - 1)
    def _():
        h = h_ref[...] + acc_sc[...]
        y_ref[...] = h * lax.rsqrt(jnp.mean(h * h, axis=-1, keepdims=True) + EPS) * gf_ref[...]


def _peer_dense(hn, rows, cols, gates, ut_bf, v_bf, h, g_final):
    n = hn.shape[0]
    tb = PEER_TOKENS
    slots = PEER_HEADS * PEER_TOPK
    tok = lambda i, c: (i, 0)
    chunk = lambda i, c: (c, 0)
    return pl.pallas_call(
        _peer_body,
        grid=(n // tb, N_EXPERTS // PEER_CHUNK),
        in_specs=[pl.BlockSpec((tb, D_MODEL), tok), pl.BlockSpec((tb, slots), tok), pl.BlockSpec((tb, slots), tok),
                  pl.BlockSpec((tb, slots), tok), pl.BlockSpec((D_MODEL, PEER_CHUNK), lambda i, c: (0, c)),
                  pl.BlockSpec((PEER_CHUNK, D_MODEL), chunk), pl.BlockSpec((tb, D_MODEL), tok),
                  pl.BlockSpec((1, D_MODEL), lambda i, c: (0, 0))],
        out_specs=pl.BlockSpec((tb, D_MODEL), tok),
        out_shape=jax.ShapeDtypeStruct((n, D_MODEL), F32),
        scratch_shapes=[pltpu.VMEM((tb * W_PITCH, LANES), F32), pltpu.VMEM((tb, D_MODEL), F32)],
        compiler_params=_params(("parallel", "arbitrary")),
        name="peer",
    )(hn, rows, cols, gates, ut_bf, v_bf, h, g_final.reshape(1, D_MODEL))


def _finish(x2d, a_out, b_out, wo_bf, g_ffn, wq_bf, sk_bf, u_bf, v_bf, g_final):
    h, hn, scores = _post_attention(x2d, a_out, b_out, wo_bf, g_ffn, wq_bf, sk_bf, min(POST_TILE, x2d.shape[0]))
    rows, cols, gates = _peer_topk(scores)
    return _peer_dense(hn, rows, cols, gates, u_bf, v_bf, h, g_final)


def kernel(x_prompt, x_sample, cache_a_k, cache_a_v, cache_b_k, cache_b_v, meta_tokens, g_attn, w_qkv,
           diff_lambda, a_norm_g, b_norm_g, w_o, g_ffn, w_peer_q, peer_sub_keys, peer_u, peer_v, g_final):
    batch, seq, _ = x_prompt.shape
    dec_batch, dec_seq, _ = x_sample.shape
    depth = w_qkv.shape[0]
    assert depth == 1, "single-layer step"
    n_past = cache_a_k.shape[2]
    lam_init = 0.8 - 0.6 * math.exp(-0.3 * 0)

    w_bf = w_qkv[0].astype(BF16)
    wo_bf = w_o[0].astype(BF16)
    wq_bf = w_peer_q[0].astype(BF16).T
    sk_bf = peer_sub_keys[0].astype(BF16)
    u_bf = peer_u[0].astype(BF16).T
    v_bf = peer_v[0].astype(BF16)
    g1, g2 = g_attn[0], g_ffn[0]
    lam_p, a_g, b_g = diff_lambda[0], a_norm_g[0], b_norm_g[0]

    xp = x_prompt.reshape(batch * seq, D_MODEL)
    xs = x_sample.reshape(dec_batch * dec_seq, D_MODEL)
    p_ak, p_av, p_bk, p_bv, p_bf = _qkv_project(xp, N_META + jnp.arange(seq, dtype=jnp.int32), g1, w_bf, ROW_TILE)
    m_ak, m_av, m_bk, m_bv, m_bf = _qkv_project(meta_tokens.astype(F32), jnp.arange(N_META, dtype=jnp.int32),
                                                g1, w_bf, N_META)
    s_ak, s_av, s_bk, s_bv, s_bf = _qkv_project(xs, n_past + jnp.arange(dec_seq, dtype=jnp.int32), g1, w_bf, dec_seq)

    meta_pad = jnp.pad(m_bf, ((0, LANES - N_META), (0, 0)))
    pa, pb = _prompt_attention(p_bf, meta_pad, batch, seq, lam_p, a_g, b_g, lam_init)
    y_prompt = _finish(xp, pa, pb, wo_bf, g2, wq_bf, sk_bf, u_bf, v_bf, g_final)

    ca_k = cache_a_k[0].reshape(dec_batch, n_past * A_HEADS, 2 * HEAD_DIM)
    ca_v = cache_a_v[0].reshape(dec_batch, n_past * A_HEADS, 2 * HEAD_DIM)
    feature_major = lambda c: jnp.transpose(c[0], (0, 2, 3, 1)).reshape(dec_batch, B_WIDTH, n_past)
    tail_rows = lambda c: c[0, :, n_past - N_META:].reshape(dec_batch, N_META, B_WIDTH)
    sa, sb = _sample_attention(s_bf, ca_k, ca_v, feature_major(cache_b_k), feature_major(cache_b_v),
                               tail_rows(cache_b_k), tail_rows(cache_b_v), dec_batch, dec_seq,
                               lam_p, a_g, b_g, lam_init)
    y_sample = _finish(xs, sa, sb, wo_bf, g2, wq_bf, sk_bf, u_bf, v_bf, g_final)

    def prompt_cache(meta_rows, frame_rows, heads):
        m = jnp.broadcast_to(meta_rows[None], (batch, N_META, A_WIDTH))
        full = jnp.concatenate([m, frame_rows.reshape(batch, seq, A_WIDTH)], axis=1)
        return full.reshape(1, batch, N_META + seq, heads, A_WIDTH // heads)

    def sample_cache(rows, heads):
        return rows.reshape(1, dec_batch, dec_seq, heads, A_WIDTH // heads)

    return (y_prompt.reshape(batch, seq, D_MODEL), y_sample.reshape(dec_batch, dec_seq, D_MODEL),
            prompt_cache(m_ak, p_ak, A_HEADS), prompt_cache(m_av, p_av, A_HEADS),
            prompt_cache(m_bk, p_bk, B_HEADS), prompt_cache(m_bv, p_bv, B_HEADS),
            sample_cache(s_ak, A_HEADS), sample_cache(s_av, A_HEADS),
            sample_cache(s_bk, B_HEADS), sample_cache(s_bv, B_HEADS))
```

```python
import functools
import math

import jax
import jax.numpy as jnp
from jax import lax
from jax.experimental import pallas as pl
from jax.experimental.pallas import tpu as pltpu

F32 = jnp.float32
BF16 = jnp.bfloat16

D_MODEL = 1024
CHUNK = 64
N_META = 16
HEAD_DIM = 64
A_HEADS = 4
B_HEADS = 8
A_WIDTH = A_HEADS * 2 * HEAD_DIM
B_WIDTH = B_HEADS * HEAD_DIM
MIX_WIDTH = A_WIDTH + B_WIDTH
QKV_WIDTH = 3 * MIX_WIDTH
ROT_DIM = HEAD_DIM // 4
ROPE_THETA = 500000.0
N_KEYS = 128
N_EXPERTS = N_KEYS * N_KEYS
PEER_HEADS = 8
PEER_TOPK = 16
D_KEY = 256
EPS = 1e-6
NEG_BIG = -1e30

LANES = 128
ROW_TILE = 256
POST_TILE = 512
ATT_BLOCK = 256
TOPK_TOKENS = 128
PEER_TOKENS = 256
PEER_CHUNK = 2048
PEER_SUB = 2048
PEER_BUILD_UNROLL = 32
W_PITCH = N_KEYS + 8
STICK_PIECE = 512
STICK_EXIT = -110.0
VMEM_LIMIT = 56 * 1024 * 1024

COL_AQ, COL_AK, COL_AV = 0, 4, 8
COL_BQ, COL_BK, COL_BV = 12, 16, 20


def _dot(a, b):
    return jnp.dot(a, b, preferred_element_type=F32)


def _dot_nt(a, b):
    return lax.dot_general(a, b, (((1,), (1,)), ((), ())), preferred_element_type=F32)


def _dot_tn(a, b):
    return lax.dot_general(a, b, (((0,), (0,)), ((), ())), preferred_element_type=F32)


def _params(sem):
    return pltpu.CompilerParams(dimension_semantics=sem, vmem_limit_bytes=VMEM_LIMIT)


def _qkv_body(x_ref, g_ref, w_ref, c_ref, s1_ref, s2_ref, ak_ref, av_ref, bk_ref, bv_ref, bf_ref):
    x = x_ref[...]
    xn = (x * lax.rsqrt(jnp.mean(x * x, axis=-1, keepdims=True) + EPS) * g_ref[...]).astype(BF16)
    cos_t = c_ref[...]
    sin_lo = s1_ref[...]
    sin_hi = s2_ref[...]
    f32_outs = {1: ak_ref, 2: av_ref, 4: bk_ref, 5: bv_ref}
    for grp in range(6):
        for half in range(2):
            y2 = _dot(xn, w_ref[:, (grp * 4 + half * 2) * LANES:(grp * 4 + half * 2 + 2) * LANES])
            for k in range(2):
                j = half * 2 + k
                cb = grp * 4 + j
                y = y2[:, k * LANES:(k + 1) * LANES]
                if grp in (0, 1):
                    y = (y * cos_t + pltpu.roll(y, LANES - ROT_DIM // 2, 1) * sin_lo
                         + pltpu.roll(y, ROT_DIM // 2, 1) * sin_hi)
                if grp in (1, 2):
                    f32_outs[grp][pl.ds(j, x.shape[0], stride=A_HEADS), :] = y
                elif grp in f32_outs:
                    f32_outs[grp][:, j * LANES:(j + 1) * LANES] = y
                if grp in (0, 3):
                    y = y * (HEAD_DIM ** -0.5)
                bf_ref[:, cb * LANES:(cb + 1) * LANES] = y.astype(BF16)


def _rope_tables(pos):
    half = ROT_DIM // 2
    inv_freq = ROPE_THETA ** (-jnp.arange(half, dtype=F32) * 2.0 / ROT_DIM)
    ang = pos.astype(F32)[:, None] * inv_freq[None, :]
    cos, sin = jnp.cos(ang), jnp.sin(ang)
    t = pos.shape[0]
    pad = jnp.zeros((t, HEAD_DIM - ROT_DIM), F32)
    zero = jnp.zeros((t, half), F32)
    cos_t = jnp.concatenate([cos, cos, pad + 1.0], axis=1)
    sin_lo = jnp.concatenate([-sin, zero, pad], axis=1)
    sin_hi = jnp.concatenate([zero, sin, pad], axis=1)
    tile = lambda a: jnp.concatenate([a, a], axis=1)
    return tile(cos_t), tile(sin_lo), tile(sin_hi)


def _qkv_project(x2d, pos, g, w_bf, tm, lead=0):
    n = x2d.shape[0]
    t = pos.shape[0]
    per = t // tm
    tabs = _rope_tables(pos)
    row = lambda b, j: (b * per + j, 0)
    tab = lambda b, j: (j, 0)
    full = lambda b, j: (0, 0)
    f32_out = jax.ShapeDtypeStruct((n, A_WIDTH), F32)
    a_rows = tm * A_HEADS
    a_out = jax.ShapeDtypeStruct(((n // t) * (lead + t) * A_HEADS, LANES), F32)
    a_spec = pl.BlockSpec((pl.Element(a_rows), pl.Element(LANES)),
                          lambda b, j: (pl.multiple_of(b * ((lead + t) * A_HEADS) + lead * A_HEADS + j * a_rows, 8),
                                        0))
    return pl.pallas_call(
        _qkv_body,
        grid=(n // t, per),
        in_specs=[pl.BlockSpec((tm, D_MODEL), row), pl.BlockSpec((1, D_MODEL), full),
                  pl.BlockSpec((D_MODEL, QKV_WIDTH), full),
                  pl.BlockSpec((tm, LANES), tab), pl.BlockSpec((tm, LANES), tab), pl.BlockSpec((tm, LANES), tab)],
        out_specs=[a_spec, a_spec] + [pl.BlockSpec((tm, A_WIDTH), row)] * 2 + [pl.BlockSpec((tm, QKV_WIDTH), row)],
        out_shape=[a_out, a_out, f32_out, f32_out, jax.ShapeDtypeStruct((n, QKV_WIDTH), BF16)],
        compiler_params=_params(("parallel", "parallel")),
        name="qkv",
    )(x2d, g.reshape(1, D_MODEL), w_bf, *tabs)


def _stack_heads_t(q):
    qt = q.astype(F32).T
    sub = lax.broadcasted_iota(jnp.int32, qt.shape, 0)
    first = jnp.where(sub < HEAD_DIM, qt, 0.0)
    second = jnp.where(sub >= HEAD_DIM, qt, 0.0)
    return jnp.concatenate([first, second], axis=1).astype(BF16)


def _softmax_init(n2):
    return jnp.full((1, n2), NEG_BIG, F32), jnp.zeros((1, n2), F32)


def _softmax_update(s, m, l):
    m_new = jnp.maximum(m, jnp.max(s, axis=0, keepdims=True))
    alpha = jnp.exp(m - m_new)
    p = jnp.exp(s - m_new)
    return m_new, alpha, alpha * l + jnp.sum(p, axis=0, keepdims=True), p.astype(BF16)


def _softmax_first(kb, vb, mask, scratch):
    qst_sc, _, _, acc_sc = scratch
    m, l = _softmax_init(qst_sc.shape[1])
    sc = jnp.where(mask, _dot(kb, qst_sc[...]), NEG_BIG)
    m, _, l, p = _softmax_update(sc, m, l)
    acc_sc[...] = _dot_tn(vb, p)
    return m, l


def _softmax_scratch(tk, n2):
    return [pltpu.VMEM((LANES, n2), BF16), pltpu.VMEM((tk, n2), F32), pltpu.VMEM((tk, n2), BF16),
            pltpu.VMEM((LANES, n2), F32)]


def _softmax_scan(n_blocks, kb_fn, vb_fn, last_mask, state, scratch):
    qst_sc, s_sc, p_sc, acc_sc = scratch
    n2 = qst_sc.shape[1]
    mm_cols = min(2 * LANES, n2)

    def block_step(m, l, v_prev, k_next, mask):
        m_out, l_out = [], []
        for c0 in range(0, n2, mm_cols):
            wide = slice(c0, c0 + mm_cols)
            pv = _dot_tn(v_prev, p_sc[:, wide])
            tiles = []
            for t0 in range(c0, c0 + mm_cols, LANES):
                cols = slice(t0, t0 + LANES)
                sc = s_sc[:, cols]
                if mask is not None:
                    sc = jnp.where(mask[:, cols], sc, NEG_BIG)
                tiles.append((cols, sc))
            if k_next is not None:
                s_sc[:, wide] = _dot(k_next, qst_sc[:, wide])
            for cols, sc in tiles:
                m_new, alpha, l_new, p = _softmax_update(sc, m[:, cols], l[:, cols])
                p_sc[:, cols] = p
                off = cols.start - c0
                acc_sc[:, cols] = alpha * (acc_sc[:, cols] + pv[:, off:off + LANES])
                m_out.append(m_new)
                l_out.append(l_new)
        return jnp.concatenate(m_out, axis=1), jnp.concatenate(l_out, axis=1)

    m, l = state
    s_sc[...] = _dot(kb_fn(0), qst_sc[...])
    p_sc[...] = jnp.zeros(p_sc.shape, BF16)
    last = n_blocks - 1

    def body(j, carry):
        return block_step(*carry, vb_fn(jnp.maximum(j - 1, 0)), kb_fn(j + 1), None)

    m, l = lax.fori_loop(0, last // 2, lambda jj, carry: body(2 * jj + 1, body(2 * jj, carry)), (m, l))
    m, l = lax.cond(last % 2 == 1, lambda carry: body(last - 1, carry), lambda carry: carry, (m, l))
    m, l = block_step(m, l, vb_fn(jnp.maximum(last - 1, 0)), None, last_mask)
    acc_sc[...] += _dot_tn(vb_fn(last), p_sc[...])
    return m, l


def _diff_finish(l, acc_sc, lam_ref, g_ref, o_ref, lam_init):
    tq = o_ref.shape[0]
    acc = acc_sc[...]
    lp = lam_ref[...]
    lam = (jnp.exp(jnp.sum(lp[0:1] * lp[1:2], axis=1, keepdims=True))
           - jnp.exp(jnp.sum(lp[2:3] * lp[3:4], axis=1, keepdims=True)) + lam_init)
    on = (acc / l).T
    o = on[:tq] - lam * on[tq:]
    o = o * lax.rsqrt(jnp.mean(o * o, axis=1, keepdims=True) + EPS) * g_ref[...] * (1.0 - lam_init)
    o_ref[...] = o.astype(o_ref.dtype)


def _stick_pieces(n2):
    width = min(STICK_PIECE, n2)
    return [slice(c0, c0 + width) for c0 in range(0, n2, width)]


def _stick_init(n2):
    return tuple((jnp.zeros((1, p.stop - p.start), F32), jnp.zeros((LANES, p.stop - p.start), F32))
                 for p in _stick_pieces(n2))


def _stick_step(qst, kb, vb, tri, mask_fn, state, kv_transposed=False):
    out = []
    for cols, (c, acc) in zip(_stick_pieces(qst.shape[1]), state):
        q = qst[:, cols]
        z = _dot_tn(kb, q) if kv_transposed else _dot(kb, q)
        lm = -(jnp.maximum(z, 0.0) + jnp.log(1.0 + jnp.exp(-jnp.abs(z))))
        if mask_fn is not None:
            mask = mask_fn(lax.broadcasted_iota(jnp.int32, z.shape, 0),
                           lax.broadcasted_iota(jnp.int32, z.shape, 1) + cols.start)
            lm = jnp.where(mask, lm, 0.0)
        hi = lm.astype(BF16)
        lo = (lm - hi.astype(F32)).astype(BF16)
        cum = _dot(tri, hi) + _dot(tri, lo)
        a = jnp.exp(z + cum + c)
        if mask_fn is not None:
            a = jnp.where(mask, a, 0.0)
        a = a.astype(BF16)
        out.append((c + cum[0:1, :], acc + (_dot(vb, a) if kv_transposed else _dot_tn(vb, a))))
    return tuple(out)


def _stick_cmax(state):
    return functools.reduce(jnp.maximum, [jnp.max(c) for c, _ in state])


def _tri(n):
    r = lax.broadcasted_iota(jnp.int32, (n, n), 0)
    c = lax.broadcasted_iota(jnp.int32, (n, n), 1)
    return jnp.where(c >= r, 1.0, 0.0).astype(BF16)


def _stick_finish(state, g_ref, o_ref):
    tq = o_ref.shape[0]
    acc = jnp.concatenate([a.T for _, a in state], axis=0)
    lane = lax.broadcasted_iota(jnp.int32, (tq, LANES), 1)
    first = lane < HEAD_DIM
    o = jnp.where(first, acc[:tq], acc[tq:])
    sq = o * o
    ss0 = jnp.sum(jnp.where(first, sq, 0.0), axis=1, keepdims=True)
    ss1 = jnp.sum(jnp.where(first, 0.0, sq), axis=1, keepdims=True)
    ms = jnp.where(first, ss0, ss1) * (1.0 / HEAD_DIM)
    o_ref[...] = (o * lax.rsqrt(ms + EPS) * g_ref[...]).astype(o_ref.dtype)


def _stick_scan(n_blocks, step_fn, state):
    def cond(st):
        return jnp.logical_and(st[0] < n_blocks, st[1] > STICK_EXIT)

    def body(st):
        state = step_fn(st[0], st[2])
        return st[0] + 1, _stick_cmax(state), state

    out = lax.while_loop(cond, body, (jnp.int32(0), _stick_cmax(state), state))
    return out[1], out[2]


def _diff_prompt_body(q_ref, k_ref, v_ref, km_ref, vm_ref, lam_ref, g_ref, o_ref, *scratch, lam_init):
    tq = q_ref.shape[0]
    i = pl.program_id(2)
    scratch[0][...] = _stack_heads_t(q_ref[...])
    nm = km_ref.shape[0]
    key = lax.broadcasted_iota(jnp.int32, (nm, 2 * tq), 0)
    state = _softmax_first(km_ref[...], vm_ref[...], key < N_META, scratch)
    key = lax.broadcasted_iota(jnp.int32, (tq, 2 * tq), 0)
    query = lax.broadcasted_iota(jnp.int32, (tq, 2 * tq), 1) % tq
    block = lambda ref: (lambda j: ref[pl.ds(pl.multiple_of(j * tq, tq), tq), :])
    _, l = _softmax_scan(i + 1, block(k_ref), block(v_ref), (key // CHUNK) <= (query // CHUNK), state, scratch)
    _diff_finish(l, scratch[3], lam_ref, g_ref, o_ref, lam_init)


def _stick_prompt_body(q_ref, k_ref, v_ref, km_ref, vm_ref, g_ref, o_ref):
    tq = q_ref.shape[0]
    i = pl.program_id(2)
    qst = _stack_heads_t(q_ref[...])
    tri = _tri(tq)
    block = lambda ref, j: ref[pl.ds(pl.multiple_of(j * tq, tq), tq), :]
    state = _stick_step(qst, block(k_ref, i), block(v_ref, i), tri, lambda key, col: key < col % tq,
                        _stick_init(2 * tq))
    cmax, state = _stick_scan(
        i, lambda n, st: _stick_step(qst, block(k_ref, i - 1 - n), block(v_ref, i - 1 - n), tri, None, st), state)
    nm = km_ref.shape[0]
    state = lax.cond(cmax > STICK_EXIT,
                     lambda st: _stick_step(qst, km_ref[...], vm_ref[...], tri[:nm, :nm],
                                            lambda key, col: key < N_META, st),
                     lambda st: st, state)
    _stick_finish(state, g_ref, o_ref)


def _prompt_attention(qkv_bf, meta_bf, batch, seq, diff_lambda, a_g, b_g, lam_init):
    tq = ATT_BLOCK
    nq = seq // tq
    n = batch * seq
    qmap = lambda off: (lambda b, h, i: (b * nq + i, off + h))
    kvmap = lambda off: (lambda b, h, i: (b, off + h))
    mmap = lambda off: (lambda b, h, i: (0, off + h))
    full = lambda b, h, i: (0, 0)
    nm = meta_bf.shape[0]

    def specs(cq, ck, cv):
        return [pl.BlockSpec((tq, LANES), qmap(cq)),
                pl.BlockSpec((seq, LANES), kvmap(ck)), pl.BlockSpec((seq, LANES), kvmap(cv)),
                pl.BlockSpec((nm, LANES), mmap(ck)), pl.BlockSpec((nm, LANES), mmap(cv))]

    out_spec = pl.BlockSpec((tq, LANES), lambda b, h, i: (b * nq + i, h))
    out_shape = jax.ShapeDtypeStruct((n, A_WIDTH), BF16)
    sem = ("parallel", "parallel", "arbitrary")
    a_out = pl.pallas_call(
        functools.partial(_diff_prompt_body, lam_init=lam_init),
        grid=(batch, A_HEADS, nq),
        in_specs=specs(COL_AQ, COL_AK, COL_AV) + [pl.BlockSpec((4, HEAD_DIM), full), pl.BlockSpec((1, LANES), full)],
        out_specs=out_spec, out_shape=out_shape, scratch_shapes=_softmax_scratch(tq, 2 * tq),
        compiler_params=_params(sem), name="diff_prompt",
    )(qkv_bf, qkv_bf, qkv_bf, meta_bf, meta_bf, diff_lambda, a_g.reshape(1, LANES))
    b_g2 = jnp.concatenate([b_g, b_g]).reshape(1, LANES)
    b_out = pl.pallas_call(
        _stick_prompt_body,
        grid=(batch, B_HEADS // 2, nq),
        in_specs=specs(COL_BQ, COL_BK, COL_BV) + [pl.BlockSpec((1, LANES), full)],
        out_specs=out_spec, out_shape=out_shape,
        compiler_params=_params(sem), name="stick_prompt",
    )(qkv_bf, qkv_bf, qkv_bf, meta_bf, meta_bf, b_g2)
    return a_out, b_out


def _last_block(tail_ref, new_ref):
    tail = tail_ref[...].astype(BF16)
    new = new_ref[...]
    pad = jnp.zeros((LANES - tail.shape[0] - new.shape[0], LANES), BF16)
    return jnp.concatenate([tail, new, pad], axis=0)


def _diff_sample_body(q_ref, kn_ref, vn_ref, kc_ref, vc_ref, kt_ref, vt_ref, lam_ref, g_ref, o_ref, *scratch,
                      lam_init):
    ts = q_ref.shape[0]
    h = pl.program_id(1)
    nt = kt_ref.shape[0] // A_HEADS
    scratch[0][...] = _stack_heads_t(q_ref[...])
    key = lax.broadcasted_iota(jnp.int32, (LANES, 2 * ts), 0)
    head_rows = lambda ref, first, n: ref[pl.ds(first * A_HEADS + h, n, stride=A_HEADS), :].astype(BF16)

    def last_block(tail_ref, new_ref):
        pad = jnp.zeros((LANES - nt - ts, LANES), BF16)
        return jnp.concatenate([head_rows(tail_ref, 0, nt), new_ref[...], pad], axis=0)

    state = _softmax_first(last_block(kt_ref, kn_ref), last_block(vt_ref, vn_ref), key < nt + ts, scratch)
    blk = ATT_BLOCK
    block = lambda ref: (lambda j: head_rows(ref, j * blk, blk))
    _, l = _softmax_scan(kc_ref.shape[0] // (A_HEADS * blk), block(kc_ref), block(vc_ref), None, state, scratch)
    _diff_finish(l, scratch[3], lam_ref, g_ref, o_ref, lam_init)


def _stick_sample_body(q_ref, kn_ref, vn_ref, kc_ref, vc_ref, kt_ref, vt_ref, g_ref, o_ref):
    ts = q_ref.shape[0]
    nt = kt_ref.shape[0]
    qst = _stack_heads_t(q_ref[...])
    blk = ATT_BLOCK
    tri = _tri(blk)
    last_mask = lambda key, col: (key < nt) | ((key < nt + ts) & (key - nt < col % ts))
    state = _stick_step(qst, _last_block(kt_ref, kn_ref), _last_block(vt_ref, vn_ref), tri[:LANES, :LANES],
                        last_mask, _stick_init(2 * ts))
    nb = kc_ref.shape[1] // blk
    block = lambda ref, j: ref[:, pl.ds(pl.multiple_of(j * blk, blk), blk)].astype(BF16)
    _, state = _stick_scan(
        nb, lambda n, st: _stick_step(qst, block(kc_ref, nb - 1 - n), block(vc_ref, nb - 1 - n), tri, None, st,
                                      kv_transposed=True),
        state)
    _stick_finish(state, g_ref, o_ref)


def _sample_attention(qkv_bf, ca_k, ca_v, cbt_k, cbt_v, cb_tail_k, cb_tail_v, dec_batch, dec_seq,
                      diff_lambda, a_g, b_g, lam_init):
    n_past = cbt_k.shape[2]
    main = n_past - N_META
    tail_blk = main // N_META
    full = lambda b, h: (0, 0)

    def new_specs(cq, ck, cv):
        return [pl.BlockSpec((dec_seq, LANES), lambda b, h: (b, cq + h)),
                pl.BlockSpec((dec_seq, LANES), lambda b, h: (b, ck + h)),
                pl.BlockSpec((dec_seq, LANES), lambda b, h: (b, cv + h))]

    b_main = pl.BlockSpec((None, LANES, main), lambda b, h: (b, h, 0))
    b_tail = pl.BlockSpec((None, N_META, LANES), lambda b, h: (b, 0, h))

    a_main = pl.BlockSpec((None, main * A_HEADS, LANES), lambda b, h: (b, 0, 0))
    a_tail = pl.BlockSpec((None, N_META * A_HEADS, LANES), lambda b, h: (b, tail_blk, 0))
    out_spec = pl.BlockSpec((dec_seq, LANES), lambda b, h: (b, h))
    out_shape = jax.ShapeDtypeStruct((dec_batch * dec_seq, A_WIDTH), BF16)
    sem = ("parallel", "parallel")
    a_out = pl.pallas_call(
        functools.partial(_diff_sample_body, lam_init=lam_init),
        grid=(dec_batch, A_HEADS),
        in_specs=(new_specs(COL_AQ, COL_AK, COL_AV) + [a_main, a_main, a_tail, a_tail]
                  + [pl.BlockSpec((4, HEAD_DIM), full), pl.BlockSpec((1, LANES), full)]),
        out_specs=out_spec, out_shape=out_shape, scratch_shapes=_softmax_scratch(ATT_BLOCK, 2 * dec_seq),
        compiler_params=_params(sem), name="diff_sample",
    )(qkv_bf, qkv_bf, qkv_bf, ca_k, ca_v, ca_k, ca_v, diff_lambda, a_g.reshape(1, LANES))
    b_g2 = jnp.concatenate([b_g, b_g]).reshape(1, LANES)
    b_out = pl.pallas_call(
        _stick_sample_body,
        grid=(dec_batch, B_HEADS // 2),
        in_specs=(new_specs(COL_BQ, COL_BK, COL_BV) + [b_main, b_main, b_tail, b_tail]
                  + [pl.BlockSpec((1, LANES), full)]),
        out_specs=out_spec, out_shape=out_shape,
        compiler_params=_params(sem), name="stick_sample",
    )(qkv_bf, qkv_bf, qkv_bf, cbt_k, cbt_v, cb_tail_k, cb_tail_v, b_g2)
    return a_out, b_out


def _post_body(x_ref, a_ref, b_ref, wo_ref, g_ref, wqt_ref, sk_ref, h_ref, hn_ref, s_ref):
    h = (x_ref[...] + _dot(a_ref[...], wo_ref[:A_WIDTH, :]) + _dot(b_ref[...], wo_ref[A_WIDTH:, :]))
    h_ref[...] = h
    hn = h * lax.rsqrt(jnp.mean(h * h, axis=-1, keepdims=True) + EPS) * g_ref[...]
    hn_ref[...] = hn.astype(BF16)
    hnt = hn.T.astype(BF16)
    half = D_KEY // 2
    for hd in range(PEER_HEADS):
        qt = _dot(wqt_ref[hd * D_KEY:(hd + 1) * D_KEY, :], hnt).astype(BF16)
        for p in range(2):
            s_ref[2 * hd + p] = _dot(sk_ref[p], qt[p * half:(p + 1) * half, :])


def _post_attention(x2d, a_out, b_out, wo_bf, g_ffn, wq_bf, sk_bf, tm):
    n = x2d.shape[0]
    row = lambda i: (i, 0)
    full = lambda i: (0, 0)
    return pl.pallas_call(
        _post_body,
        grid=(n // tm,),
        in_specs=[pl.BlockSpec((tm, D_MODEL), row), pl.BlockSpec((tm, A_WIDTH), row), pl.BlockSpec((tm, B_WIDTH), row),
                  pl.BlockSpec((MIX_WIDTH, D_MODEL), full), pl.BlockSpec((1, D_MODEL), full),
                  pl.BlockSpec((PEER_HEADS * D_KEY, D_MODEL), full),
                  pl.BlockSpec((2, N_KEYS, D_KEY // 2), lambda i: (0, 0, 0))],
        out_specs=[pl.BlockSpec((tm, D_MODEL), row), pl.BlockSpec((tm, D_MODEL), row),
                   pl.BlockSpec((2 * PEER_HEADS, N_KEYS, tm), lambda i: (0, 0, i))],
        out_shape=[jax.ShapeDtypeStruct((n, D_MODEL), F32), jax.ShapeDtypeStruct((n, D_MODEL), BF16),
                   jax.ShapeDtypeStruct((2 * PEER_HEADS, N_KEYS, n), F32)],
        compiler_params=_params(("parallel",)),
        name="post",
    )(x2d, a_out, b_out, wo_bf, g_ffn.reshape(1, D_MODEL), wq_bf, sk_bf)


def _extract_top(s, iota, take, val_ref, aux, aux_ref):
    n_rows = float(s.shape[0])
    for r in range(take):
        m = jnp.max(s, axis=0, keepdims=True)
        pos = jnp.min(jnp.where(s == m, iota, n_rows), axis=0, keepdims=True)
        sel = iota == pos
        val_ref[r:r + 1, :] = m
        if aux is None:
            aux_ref[r:r + 1, :] = pos
        else:
            aux_ref[r:r + 1, :] = jnp.sum(jnp.where(sel, aux, 0.0), axis=0, keepdims=True)
        s = jnp.where(sel, -jnp.inf, s)


def _topk_body(s_ref, row_ref, col_ref, gate_ref, v0_ref, i0_ref, v1_ref, i1_ref, bs_ref, id_ref,
               rows_sc, cols_sc, gates_sc):
    tt = s_ref.shape[2]
    k = PEER_TOPK
    key_iota = lax.broadcasted_iota(jnp.int32, (N_KEYS, tt), 0).astype(F32)
    n_cand = k + 8 * 7 + 8
    cand_iota = lax.broadcasted_iota(jnp.int32, (n_cand, tt), 0).astype(F32)

    def head(h, carry):
        _extract_top(s_ref[2 * h], key_iota, k, v0_ref, None, i0_ref)
        _extract_top(s_ref[2 * h + 1], key_iota, k, v1_ref, None, i1_ref)
        v1_all, i1_all = v1_ref[...], i1_ref[...]
        cs = [v0_ref[0:1, :] + v1_all]
        ci = [i0_ref[0:1, :] * N_KEYS + i1_all]
        for r in range(1, 8):
            cs.append(v0_ref[r:r + 1, :] + v1_all[:8])
            ci.append(i0_ref[r:r + 1, :] * N_KEYS + i1_all[:8])
        cs.append(v0_ref[8:16, :] + v1_all[0:1])
        ci.append(i0_ref[8:16, :] * N_KEYS + i1_all[0:1])
        _extract_top(jnp.concatenate(cs, axis=0), cand_iota, k, bs_ref, jnp.concatenate(ci, axis=0), id_ref)
        best = bs_ref[...]
        e = jnp.exp(best - best[0:1])
        ids = id_ref[...]
        rows = jnp.floor(ids * (1.0 / N_KEYS))
        dst = pl.ds(pl.multiple_of(h * k, k), k)
        gates_sc[dst, :] = e / jnp.sum(e, axis=0, keepdims=True)
        rows_sc[dst, :] = rows
        cols_sc[dst, :] = ids - rows * N_KEYS
        return carry

    lax.fori_loop(0, PEER_HEADS, head, 0, unroll=4)
    row_ref[...] = rows_sc[...].T
    col_ref[...] = cols_sc[...].T
    gate_ref[...] = gates_sc[...].T


def _peer_topk(scores):
    n = scores.shape[2]
    tt = TOPK_TOKENS
    slots = PEER_HEADS * PEER_TOPK
    out = jax.ShapeDtypeStruct((n, slots), F32)
    return pl.pallas_call(
        _topk_body,
        grid=(n // tt,),
        in_specs=[pl.BlockSpec((2 * PEER_HEADS, N_KEYS, tt), lambda i: (0, 0, i))],
        out_specs=[pl.BlockSpec((tt, slots), lambda i: (i, 0))] * 3,
        out_shape=[out] * 3,
        scratch_shapes=[pltpu.VMEM((PEER_TOPK, tt), F32)] * 6 + [pltpu.VMEM((slots, tt), F32)] * 3,
        compiler_params=_params(("parallel",)),
        name="topk",
    )(scores)


def _gelu_x2(x):
    return x * (1.0 + lax.erf(x * (2.0 ** -0.5)))


def _peer_body(hn_ref, row_ref, col_ref, gate_ref, ut_ref, v_ref, h_ref, gf_ref, y_ref, w_sc, acc_sc):
    tb = hn_ref.shape[0]
    c = pl.program_id(1)
    per_chunk = v_ref.shape[0] // N_KEYS

    @pl.when(c == 0)
    def _():
        acc_sc[...] = jnp.zeros_like(acc_sc)
        sub = lax.broadcasted_iota(jnp.int32, (N_KEYS, LANES), 0).astype(F32)

        def build(t, carry):
            r = row_ref[pl.ds(t, 1), :]
            cc = col_ref[pl.ds(t, 1), :]
            g = gate_ref[pl.ds(t, 1), :] * 0.5
            at = jnp.where(sub == r, g, 0.0).astype(BF16)
            bt = jnp.where(sub == cc, 1.0, 0.0).astype(BF16)
            w_sc[pl.ds(pl.multiple_of(t * W_PITCH, 8), N_KEYS), :] = _dot_nt(at, bt)
            return carry

        lax.fori_loop(0, tb, build, 0, unroll=PEER_BUILD_UNROLL)

    x = hn_ref[...]
    total = None
    for s in range(v_ref.shape[0] // PEER_SUB):
        experts = slice(s * PEER_SUB, (s + 1) * PEER_SUB)
        pre = _dot(x, ut_ref[:, experts])
        first = c * per_chunk + s * (PEER_SUB // N_KEYS)
        w = jnp.concatenate([w_sc[pl.ds(first + a, tb, stride=W_PITCH), :] for a in range(PEER_SUB // N_KEYS)],
                            axis=1)
        part = _dot((_gelu_x2(pre) * w).astype(BF16), v_ref[experts, :])
        total = part if total is None else total + part
    acc_sc[...] += total

    @pl.when(c == pl.num_programs(1) - 1)
    def _():
        h = h_ref[...] + acc_sc[...]
        y_ref[...] = h * lax.rsqrt(jnp.mean(h * h, axis=-1, keepdims=True) + EPS) * gf_ref[...]


def _peer_dense(hn, rows, cols, gates, ut_bf, v_bf, h, g_final):
    n = hn.shape[0]
    tb = PEER_TOKENS
    slots = PEER_HEADS * PEER_TOPK
    tok = lambda i, c: (i, 0)
    chunk = lambda i, c: (c, 0)
    return pl.pallas_call(
        _peer_body,
        grid=(n // tb, N_EXPERTS // PEER_CHUNK),
        in_specs=[pl.BlockSpec((tb, D_MODEL), tok), pl.BlockSpec((tb, slots), tok), pl.BlockSpec((tb, slots), tok),
                  pl.BlockSpec((tb, slots), tok), pl.BlockSpec((D_MODEL, PEER_CHUNK), lambda i, c: (0, c)),
                  pl.BlockSpec((PEER_CHUNK, D_MODEL), chunk), pl.BlockSpec((tb, D_MODEL), tok),
                  pl.BlockSpec((1, D_MODEL), lambda i, c: (0, 0))],
        out_specs=pl.BlockSpec((tb, D_MODEL), tok),
        out_shape=jax.ShapeDtypeStruct((n, D_MODEL), F32),
        scratch_shapes=[pltpu.VMEM((tb * W_PITCH, LANES), F32), pltpu.VMEM((tb, D_MODEL), F32)],
        compiler_params=_params(("parallel", "arbitrary")),
        name="peer",
    )(hn, rows, cols, gates, ut_bf, v_bf, h, g_final.reshape(1, D_MODEL))


def _finish(x2d, a_out, b_out, wo_bf, g_ffn, wq_bf, sk_bf, u_bf, v_bf, g_final):
    h, hn, scores = _post_attention(x2d, a_out, b_out, wo_bf, g_ffn, wq_bf, sk_bf, min(POST_TILE, x2d.shape[0]))
    rows, cols, gates = _peer_topk(scores)
    return _peer_dense(hn, rows, cols, gates, u_bf, v_bf, h, g_final)


def kernel(x_prompt, x_sample, cache_a_k, cache_a_v, cache_b_k, cache_b_v, meta_tokens, g_attn, w_qkv,
           diff_lambda, a_norm_g, b_norm_g, w_o, g_ffn, w_peer_q, peer_sub_keys, peer_u, peer_v, g_final):
    batch, seq, _ = x_prompt.shape
    dec_batch, dec_seq, _ = x_sample.shape
    depth = w_qkv.shape[0]
    assert depth == 1, "single-layer step"
    n_past = cache_a_k.shape[2]
    lam_init = 0.8 - 0.6 * math.exp(-0.3 * 0)

    w_bf = w_qkv[0].astype(BF16)
    wo_bf = w_o[0].astype(BF16)
    wq_bf = w_peer_q[0].astype(BF16).T
    sk_bf = peer_sub_keys[0].astype(BF16)
    u_bf = peer_u[0].astype(BF16).T
    v_bf = peer_v[0].astype(BF16)
    g1, g2 = g_attn[0], g_ffn[0]
    lam_p, a_g, b_g = diff_lambda[0], a_norm_g[0], b_norm_g[0]

    xp = x_prompt.reshape(batch * seq, D_MODEL)
    xs = x_sample.reshape(dec_batch * dec_seq, D_MODEL)
    p_ak, p_av, p_bk, p_bv, p_bf = _qkv_project(xp, N_META + jnp.arange(seq, dtype=jnp.int32), g1, w_bf, ROW_TILE,
                                                lead=N_META)
    m_ak, m_av, m_bk, m_bv, m_bf = _qkv_project(meta_tokens.astype(F32), jnp.arange(N_META, dtype=jnp.int32),
                                                g1, w_bf, N_META)
    s_ak, s_av, s_bk, s_bv, s_bf = _qkv_project(xs, n_past + jnp.arange(dec_seq, dtype=jnp.int32), g1, w_bf, dec_seq)

    meta_pad = jnp.pad(m_bf, ((0, LANES - N_META), (0, 0)))
    pa, pb = _prompt_attention(p_bf, meta_pad, batch, seq, lam_p, a_g, b_g, lam_init)
    y_prompt = _finish(xp, pa, pb, wo_bf, g2, wq_bf, sk_bf, u_bf, v_bf, g_final)

    ca_k = cache_a_k[0].reshape(dec_batch, n_past * A_HEADS, 2 * HEAD_DIM)
    ca_v = cache_a_v[0].reshape(dec_batch, n_past * A_HEADS, 2 * HEAD_DIM)
    feature_major = lambda c: jnp.transpose(c[0], (0, 2, 3, 1)).reshape(dec_batch, B_WIDTH, n_past)
    tail_rows = lambda c: c[0, :, n_past - N_META:].reshape(dec_batch, N_META, B_WIDTH)
    sa, sb = _sample_attention(s_bf, ca_k, ca_v, feature_major(cache_b_k), feature_major(cache_b_v),
                               tail_rows(cache_b_k), tail_rows(cache_b_v), dec_batch, dec_seq,
                               lam_p, a_g, b_g, lam_init)
    y_sample = _finish(xs, sa, sb, wo_bf, g2, wq_bf, sk_bf, u_bf, v_bf, g_final)

    def prompt_cache_a(meta_rows, rows):
        full = rows.reshape(batch, (N_META + seq) * A_HEADS, 2 * HEAD_DIM)
        full = lax.dynamic_update_slice(full, jnp.broadcast_to(meta_rows[None], (batch,) + meta_rows.shape), (0, 0, 0))
        return full.reshape(1, batch, N_META + seq, A_HEADS, 2 * HEAD_DIM)

    def prompt_cache_b(meta_rows, frame_rows):
        m = jnp.broadcast_to(meta_rows[None], (batch, N_META, B_WIDTH))
        full = jnp.concatenate([m, frame_rows.reshape(batch, seq, B_WIDTH)], axis=1)
        return full.reshape(1, batch, N_META + seq, B_HEADS, HEAD_DIM)

    sample_cache = lambda rows, heads: rows.reshape(1, dec_batch, dec_seq, heads, A_WIDTH // heads)
    return (y_prompt.reshape(batch, seq, D_MODEL), y_sample.reshape(dec_batch, dec_seq, D_MODEL),
            prompt_cache_a(m_ak, p_ak), prompt_cache_a(m_av, p_av),
            prompt_cache_b(m_bk, p_bk), prompt_cache_b(m_bv, p_bv),
            sample_cache(s_ak, A_HEADS), sample_cache(s_av, A_HEADS),
            sample_cache(s_bk, B_HEADS), sample_cache(s_bv, B_HEADS))
```

```python
import functools
import math

import jax
import jax.numpy as jnp
from jax import lax
from jax.experimental import pallas as pl
from jax.experimental.pallas import tpu as pltpu

F32 = jnp.float32
BF16 = jnp.bfloat16

D_MODEL = 1024
CHUNK = 64
N_META = 16
HEAD_DIM = 64
A_HEADS = 4
B_HEADS = 8
A_WIDTH = A_HEADS * 2 * HEAD_DIM
B_WIDTH = B_HEADS * HEAD_DIM
MIX_WIDTH = A_WIDTH + B_WIDTH
QKV_WIDTH = 3 * MIX_WIDTH
ROT_DIM = HEAD_DIM // 4
ROPE_THETA = 500000.0
N_KEYS = 128
N_EXPERTS = N_KEYS * N_KEYS
PEER_HEADS = 8
PEER_TOPK = 16
D_KEY = 256
EPS = 1e-6
NEG_BIG = -1e30

LANES = 128
ROW_TILE = 256
POST_TILE = 512
ATT_BLOCK = 256
TOPK_TOKENS = 128
PEER_TOKENS = 256
PEER_CHUNK = 2048
PEER_SUB = 2048
PEER_BUILD_UNROLL = 32
W_PITCH = N_KEYS + 8
STICK_CHAINS = 4
DIFF_CHAINS = 1
STICK_EXIT = -110.0
VMEM_LIMIT = 56 * 1024 * 1024

COL_AQ, COL_AK, COL_AV = 0, 4, 8
COL_BQ, COL_BK, COL_BV = 12, 16, 20


def _dot(a, b):
    return jnp.dot(a, b, preferred_element_type=F32)


def _dot_nt(a, b):
    return lax.dot_general(a, b, (((1,), (1,)), ((), ())), preferred_element_type=F32)


def _dot_tn(a, b):
    return lax.dot_general(a, b, (((0,), (0,)), ((), ())), preferred_element_type=F32)


def _params(sem):
    return pltpu.CompilerParams(dimension_semantics=sem, vmem_limit_bytes=VMEM_LIMIT)


def _qkv_body(x_ref, g_ref, w_ref, c_ref, s1_ref, s2_ref, ak_ref, av_ref, bk_ref, bv_ref, bf_ref):
    x = x_ref[...]
    xn = (x * lax.rsqrt(jnp.mean(x * x, axis=-1, keepdims=True) + EPS) * g_ref[...]).astype(BF16)
    cos_t = c_ref[...]
    sin_lo = s1_ref[...]
    sin_hi = s2_ref[...]
    f32_outs = {1: ak_ref, 2: av_ref, 4: bk_ref, 5: bv_ref}
    for grp in range(6):
        for half in range(2):
            y2 = _dot(xn, w_ref[:, (grp * 4 + half * 2) * LANES:(grp * 4 + half * 2 + 2) * LANES])
            for k in range(2):
                j = half * 2 + k
                cb = grp * 4 + j
                y = y2[:, k * LANES:(k + 1) * LANES]
                if grp in (0, 1):
                    y = (y * cos_t + pltpu.roll(y, LANES - ROT_DIM // 2, 1) * sin_lo
                         + pltpu.roll(y, ROT_DIM // 2, 1) * sin_hi)
                if grp in (1, 2):
                    f32_outs[grp][pl.ds(j, x.shape[0], stride=A_HEADS), :] = y
                elif grp in f32_outs:
                    f32_outs[grp][:, j * LANES:(j + 1) * LANES] = y
                if grp in (0, 3):
                    y = y * (HEAD_DIM ** -0.5)
                bf_ref[:, cb * LANES:(cb + 1) * LANES] = y.astype(BF16)


def _rope_tables(pos):
    half = ROT_DIM // 2
    inv_freq = ROPE_THETA ** (-jnp.arange(half, dtype=F32) * 2.0 / ROT_DIM)
    ang = pos.astype(F32)[:, None] * inv_freq[None, :]
    cos, sin = jnp.cos(ang), jnp.sin(ang)
    t = pos.shape[0]
    pad = jnp.zeros((t, HEAD_DIM - ROT_DIM), F32)
    zero = jnp.zeros((t, half), F32)
    cos_t = jnp.concatenate([cos, cos, pad + 1.0], axis=1)
    sin_lo = jnp.concatenate([-sin, zero, pad], axis=1)
    sin_hi = jnp.concatenate([zero, sin, pad], axis=1)
    tile = lambda a: jnp.concatenate([a, a], axis=1)
    return tile(cos_t), tile(sin_lo), tile(sin_hi)


def _qkv_project(x2d, pos, g, w_bf, tm, lead=0):
    n = x2d.shape[0]
    t = pos.shape[0]
    per = t // tm
    tabs = _rope_tables(pos)
    row = lambda b, j: (b * per + j, 0)
    tab = lambda b, j: (j, 0)
    full = lambda b, j: (0, 0)
    f32_out = jax.ShapeDtypeStruct((n, A_WIDTH), F32)
    a_rows = tm * A_HEADS
    a_out = jax.ShapeDtypeStruct(((n // t) * (lead + t) * A_HEADS, LANES), F32)
    a_spec = pl.BlockSpec((pl.Element(a_rows), pl.Element(LANES)),
                          lambda b, j: (pl.multiple_of(b * ((lead + t) * A_HEADS) + lead * A_HEADS + j * a_rows, 8),
                                        0))
    return pl.pallas_call(
        _qkv_body,
        grid=(n // t, per),
        in_specs=[pl.BlockSpec((tm, D_MODEL), row), pl.BlockSpec((1, D_MODEL), full),
                  pl.BlockSpec((D_MODEL, QKV_WIDTH), full),
                  pl.BlockSpec((tm, LANES), tab), pl.BlockSpec((tm, LANES), tab), pl.BlockSpec((tm, LANES), tab)],
        out_specs=[a_spec, a_spec] + [pl.BlockSpec((tm, A_WIDTH), row)] * 2 + [pl.BlockSpec((tm, QKV_WIDTH), row)],
        out_shape=[a_out, a_out, f32_out, f32_out, jax.ShapeDtypeStruct((n, QKV_WIDTH), BF16)],
        compiler_params=_params(("parallel", "parallel")),
        name="qkv",
    )(x2d, g.reshape(1, D_MODEL), w_bf, *tabs)


def _stack_heads_t(q):
    qt = q.astype(F32).T
    sub = lax.broadcasted_iota(jnp.int32, qt.shape, 0)
    first = jnp.where(sub < HEAD_DIM, qt, 0.0)
    second = jnp.where(sub >= HEAD_DIM, qt, 0.0)
    return jnp.concatenate([first, second], axis=1).astype(BF16)


def _softmax_init(n2):
    return jnp.full((1, n2), NEG_BIG, F32), jnp.zeros((1, n2), F32)


def _softmax_update(s, m, l):
    m_new = jnp.maximum(m, jnp.max(s, axis=0, keepdims=True))
    alpha = jnp.exp(m - m_new)
    p = jnp.exp(s - m_new)
    return m_new, alpha, alpha * l + jnp.sum(p, axis=0, keepdims=True), p.astype(BF16)


def _softmax_first(kbs, vbs, mask, chains):
    state = []
    for kb, vb, (qst_sc, _, _, acc_sc) in zip(kbs, vbs, chains):
        m, l = _softmax_init(qst_sc.shape[1])
        sc = jnp.where(mask, _dot(kb, qst_sc[...]), NEG_BIG)
        m, _, l, p = _softmax_update(sc, m, l)
        acc_sc[...] = _dot_tn(vb, p)
        state.append((m, l))
    return tuple(state)


def _softmax_scratch(tk, n2):
    return [pltpu.VMEM((LANES, n2), BF16), pltpu.VMEM((tk, n2), F32), pltpu.VMEM((tk, n2), BF16),
            pltpu.VMEM((LANES, n2), F32)] * DIFF_CHAINS


def _chain_scratch(scratch):
    return [scratch[4 * ch:4 * ch + 4] for ch in range(len(scratch) // 4)]


def _softmax_scan(n_blocks, kb_fn, vb_fn, last_mask, state, chains):
    n2 = chains[0][0].shape[1]
    mm_cols = min(2 * LANES, n2)

    def block_step(state, v_prevs, k_nexts, mask):
        stats = [([], []) for _ in chains]
        for c0 in range(0, n2, mm_cols):
            wide = slice(c0, c0 + mm_cols)
            pvs = [_dot_tn(v_prev, p_sc[:, wide]) for v_prev, (_, _, p_sc, _) in zip(v_prevs, chains)]
            tiles = []
            for ch, (_, s_sc, _, _) in enumerate(chains):
                for t0 in range(c0, c0 + mm_cols, LANES):
                    cols = slice(t0, t0 + LANES)
                    sc = s_sc[:, cols]
                    if mask is not None:
                        sc = jnp.where(mask[:, cols], sc, NEG_BIG)
                    tiles.append((ch, cols, sc))
            if k_nexts is not None:
                for k_next, (qst_sc, s_sc, _, _) in zip(k_nexts, chains):
                    s_sc[:, wide] = _dot(k_next, qst_sc[:, wide])
            for ch, cols, sc in tiles:
                _, _, p_sc, acc_sc = chains[ch]
                m, l = state[ch]
                m_new, alpha, l_new, p = _softmax_update(sc, m[:, cols], l[:, cols])
                p_sc[:, cols] = p
                off = cols.start - c0
                acc_sc[:, cols] = alpha * (acc_sc[:, cols] + pvs[ch][:, off:off + LANES])
                stats[ch][0].append(m_new)
                stats[ch][1].append(l_new)
        return tuple((jnp.concatenate(ms, axis=1), jnp.concatenate(ls, axis=1)) for ms, ls in stats)

    for kb, (qst_sc, s_sc, p_sc, _) in zip(kb_fn(0), chains):
        s_sc[...] = _dot(kb, qst_sc[...])
        p_sc[...] = jnp.zeros(p_sc.shape, BF16)
    last = n_blocks - 1

    def body(j, carry):
        return block_step(carry, vb_fn(jnp.maximum(j - 1, 0)), kb_fn(j + 1), None)

    state = lax.fori_loop(0, last // 2, lambda jj, carry: body(2 * jj + 1, body(2 * jj, carry)), state)
    state = lax.cond(last % 2 == 1, lambda carry: body(last - 1, carry), lambda carry: carry, state)
    state = block_step(state, vb_fn(jnp.maximum(last - 1, 0)), None, last_mask)
    for vb, (_, _, p_sc, acc_sc) in zip(vb_fn(last), chains):
        acc_sc[...] += _dot_tn(vb, p_sc[...])
    return state


def _diff_finish(state, chains, lam_ref, g_ref, o_ref, lam_init):
    for ch, ((_, l), (_, _, _, acc_sc)) in enumerate(zip(state, chains)):
        _diff_finish_one(l, acc_sc, lam_ref, g_ref, o_ref.at[:, ch * LANES:(ch + 1) * LANES], lam_init)


def _diff_finish_one(l, acc_sc, lam_ref, g_ref, o_ref, lam_init):
    tq = o_ref.shape[0]
    acc = acc_sc[...]
    lp = lam_ref[...]
    lam = (jnp.exp(jnp.sum(lp[0:1] * lp[1:2], axis=1, keepdims=True))
           - jnp.exp(jnp.sum(lp[2:3] * lp[3:4], axis=1, keepdims=True)) + lam_init)
    on = (acc / l).T
    o = on[:tq] - lam * on[tq:]
    o = o * lax.rsqrt(jnp.mean(o * o, axis=1, keepdims=True) + EPS) * g_ref[...] * (1.0 - lam_init)
    o_ref[...] = o.astype(o_ref.dtype)


def _stick_init(n2):
    return jnp.zeros((1, n2), F32), jnp.zeros((LANES, n2), F32)


def _stick_step(qsts, kbs, vbs, tri, mask_fn, state, kv_transposed=False):
    zs = [(_dot_tn(kb, q) if kv_transposed else _dot(kb, q)) for q, kb in zip(qsts, kbs)]
    mask = None
    if mask_fn is not None:
        mask = mask_fn(lax.broadcasted_iota(jnp.int32, zs[0].shape, 0),
                       lax.broadcasted_iota(jnp.int32, zs[0].shape, 1))
    cums = []
    for z in zs:
        lm = -(jnp.maximum(z, 0.0) + jnp.log(1.0 + jnp.exp(-jnp.abs(z))))
        if mask is not None:
            lm = jnp.where(mask, lm, 0.0)
        hi = lm.astype(BF16)
        lo = (lm - hi.astype(F32)).astype(BF16)
        cums.append(_dot(tri, hi) + _dot(tri, lo))
    out = []
    for z, cum, vb, (c, acc) in zip(zs, cums, vbs, state):
        a = jnp.exp(z + cum + c)
        if mask is not None:
            a = jnp.where(mask, a, 0.0)
        a = a.astype(BF16)
        out.append((c + cum[0:1, :], acc + (_dot(vb, a) if kv_transposed else _dot_tn(vb, a))))
    return tuple(out)


def _stick_cmax(state):
    return functools.reduce(jnp.maximum, [jnp.max(c) for c, _ in state])


def _tri(n):
    r = lax.broadcasted_iota(jnp.int32, (n, n), 0)
    c = lax.broadcasted_iota(jnp.int32, (n, n), 1)
    return jnp.where(c >= r, 1.0, 0.0).astype(BF16)


def _stick_finish(state, g_ref, o_ref):
    tq = o_ref.shape[0]
    lane = lax.broadcasted_iota(jnp.int32, (tq, LANES), 1)
    first = lane < HEAD_DIM
    for ch, (_, acc) in enumerate(state):
        acc = acc.T
        o = jnp.where(first, acc[:tq], acc[tq:])
        sq = o * o
        ss0 = jnp.sum(jnp.where(first, sq, 0.0), axis=1, keepdims=True)
        ss1 = jnp.sum(jnp.where(first, 0.0, sq), axis=1, keepdims=True)
        ms = jnp.where(first, ss0, ss1) * (1.0 / HEAD_DIM)
        o_ref[:, ch * LANES:(ch + 1) * LANES] = (o * lax.rsqrt(ms + EPS) * g_ref[...]).astype(o_ref.dtype)


def _stick_scan(n_blocks, step_fn, state):
    def cond(st):
        return jnp.logical_and(st[0] < n_blocks, st[1] > STICK_EXIT)

    def body(st):
        state = step_fn(st[0], st[2])
        return st[0] + 1, _stick_cmax(state), state

    out = lax.while_loop(cond, body, (jnp.int32(0), _stick_cmax(state), state))
    return out[1], out[2]


def _diff_prompt_body(q_ref, k_ref, v_ref, km_ref, vm_ref, lam_ref, g_ref, o_ref, *scratch, lam_init):
    tq = q_ref.shape[0]
    i = pl.program_id(2)
    chains = _chain_scratch(scratch)
    heads = range(len(chains))
    part = lambda x, ch: x[:, ch * LANES:(ch + 1) * LANES]
    for ch in heads:
        chains[ch][0][...] = _stack_heads_t(part(q_ref[...], ch))
    nm = km_ref.shape[0]
    key = lax.broadcasted_iota(jnp.int32, (nm, 2 * tq), 0)
    meta = lambda ref: [part(ref[...], ch) for ch in heads]
    state = _softmax_first(meta(km_ref), meta(vm_ref), key < N_META, chains)
    key = lax.broadcasted_iota(jnp.int32, (tq, 2 * tq), 0)
    query = lax.broadcasted_iota(jnp.int32, (tq, 2 * tq), 1) % tq
    block = lambda ref: (lambda j: [ref[pl.ds(pl.multiple_of(j * tq, tq), tq), ch * LANES:(ch + 1) * LANES]
                                    for ch in heads])
    state = _softmax_scan(i + 1, block(k_ref), block(v_ref), (key // CHUNK) <= (query // CHUNK), state, chains)
    _diff_finish(state, chains, lam_ref, g_ref, o_ref, lam_init)


def _stick_prompt_body(q_ref, k_ref, v_ref, km_ref, vm_ref, g_ref, o_ref):
    tq = q_ref.shape[0]
    i = pl.program_id(2)
    chains = range(q_ref.shape[1] // LANES)
    part = lambda x, ch: x[:, ch * LANES:(ch + 1) * LANES]
    qsts = [_stack_heads_t(part(q_ref[...], ch)) for ch in chains]
    tri = _tri(tq)
    blocks = lambda ref, j: [ref[pl.ds(pl.multiple_of(j * tq, tq), tq), ch * LANES:(ch + 1) * LANES] for ch in chains]
    state = _stick_step(qsts, blocks(k_ref, i), blocks(v_ref, i), tri, lambda key, col: key < col % tq,
                        tuple(_stick_init(2 * tq) for _ in chains))
    cmax, state = _stick_scan(
        i, lambda n, st: _stick_step(qsts, blocks(k_ref, i - 1 - n), blocks(v_ref, i - 1 - n), tri, None, st), state)
    nm = km_ref.shape[0]
    meta = lambda ref: [part(ref[...], ch) for ch in chains]
    state = lax.cond(cmax > STICK_EXIT,
                     lambda st: _stick_step(qsts, meta(km_ref), meta(vm_ref), tri[:nm, :nm],
                                            lambda key, col: key < N_META, st),
                     lambda st: st, state)
    _stick_finish(state, g_ref, o_ref)


def _prompt_attention(qkv_bf, meta_bf, batch, seq, diff_lambda, a_g, b_g, lam_init):
    tq = ATT_BLOCK
    nq = seq // tq
    n = batch * seq
    full = lambda b, g, i: (0, 0)
    nm = meta_bf.shape[0]

    def specs(cq, ck, cv, chains):
        w = chains * LANES
        col = lambda off: off // chains
        return ([pl.BlockSpec((tq, w), lambda b, g, i: (b * nq + i, col(cq) + g)),
                 pl.BlockSpec((seq, w), lambda b, g, i: (b, col(ck) + g)),
                 pl.BlockSpec((seq, w), lambda b, g, i: (b, col(cv) + g)),
                 pl.BlockSpec((nm, w), lambda b, g, i: (0, col(ck) + g)),
                 pl.BlockSpec((nm, w), lambda b, g, i: (0, col(cv) + g))],
                pl.BlockSpec((tq, w), lambda b, g, i: (b * nq + i, g)))

    out_shape = jax.ShapeDtypeStruct((n, A_WIDTH), BF16)
    sem = ("parallel", "parallel", "arbitrary")
    a_in, a_out_spec = specs(COL_AQ, COL_AK, COL_AV, DIFF_CHAINS)
    a_out = pl.pallas_call(
        functools.partial(_diff_prompt_body, lam_init=lam_init),
        grid=(batch, A_HEADS // DIFF_CHAINS, nq),
        in_specs=a_in + [pl.BlockSpec((4, HEAD_DIM), full), pl.BlockSpec((1, LANES), full)],
        out_specs=a_out_spec, out_shape=out_shape, scratch_shapes=_softmax_scratch(tq, 2 * tq),
        compiler_params=_params(sem), name="diff_prompt",
    )(qkv_bf, qkv_bf, qkv_bf, meta_bf, meta_bf, diff_lambda, a_g.reshape(1, LANES))
    b_g2 = jnp.concatenate([b_g, b_g]).reshape(1, LANES)
    b_in, b_out_spec = specs(COL_BQ, COL_BK, COL_BV, STICK_CHAINS)
    b_out = pl.pallas_call(
        _stick_prompt_body,
        grid=(batch, B_HEADS // 2 // STICK_CHAINS, nq),
        in_specs=b_in + [pl.BlockSpec((1, LANES), full)],
        out_specs=b_out_spec, out_shape=out_shape,
        compiler_params=_params(sem), name="stick_prompt",
    )(qkv_bf, qkv_bf, qkv_bf, meta_bf, meta_bf, b_g2)
    return a_out, b_out


def _last_block(tail_ref, new_ref):
    tail = tail_ref[...].astype(BF16)
    new = new_ref[...]
    pad = jnp.zeros((LANES - tail.shape[0] - new.shape[0], LANES), BF16)
    return jnp.concatenate([tail, new, pad], axis=0)


def _diff_sample_body(q_ref, kn_ref, vn_ref, kc_ref, vc_ref, kt_ref, vt_ref, lam_ref, g_ref, o_ref, *scratch,
                      lam_init):
    ts = q_ref.shape[0]
    chains = _chain_scratch(scratch)
    heads = range(len(chains))
    h0 = pl.program_id(1) * len(chains)
    nt = kt_ref.shape[0] // A_HEADS
    part = lambda x, ch: x[:, ch * LANES:(ch + 1) * LANES]
    for ch in heads:
        chains[ch][0][...] = _stack_heads_t(part(q_ref[...], ch))
    key = lax.broadcasted_iota(jnp.int32, (LANES, 2 * ts), 0)
    head_rows = lambda ref, ch, first, n: ref[pl.ds(first * A_HEADS + h0 + ch, n, stride=A_HEADS), :].astype(BF16)
    pad = jnp.zeros((LANES - nt - ts, LANES), BF16)
    last = lambda tail_ref, new_ref: [
        jnp.concatenate([head_rows(tail_ref, ch, 0, nt), part(new_ref[...], ch), pad], axis=0) for ch in heads]
    state = _softmax_first(last(kt_ref, kn_ref), last(vt_ref, vn_ref), key < nt + ts, chains)
    blk = ATT_BLOCK
    block = lambda ref: (lambda j: [head_rows(ref, ch, j * blk, blk) for ch in heads])
    state = _softmax_scan(kc_ref.shape[0] // (A_HEADS * blk), block(kc_ref), block(vc_ref), None, state, chains)
    _diff_finish(state, chains, lam_ref, g_ref, o_ref, lam_init)


def _stick_sample_body(q_ref, kn_ref, vn_ref, kc_ref, vc_ref, kt_ref, vt_ref, g_ref, o_ref):
    ts = q_ref.shape[0]
    nt = kt_ref.shape[0]
    chains = range(q_ref.shape[1] // LANES)
    part = lambda x, ch: x[:, ch * LANES:(ch + 1) * LANES]
    qsts = [_stack_heads_t(part(q_ref[...], ch)) for ch in chains]
    blk = ATT_BLOCK
    tri = _tri(blk)
    pad = jnp.zeros((LANES - nt - ts, LANES), BF16)
    last = lambda tail_ref, new_ref: [
        jnp.concatenate([part(tail_ref[...], ch).astype(BF16), part(new_ref[...], ch), pad], axis=0) for ch in chains]
    last_mask = lambda key, col: (key < nt) | ((key < nt + ts) & (key - nt < col % ts))
    state = _stick_step(qsts, last(kt_ref, kn_ref), last(vt_ref, vn_ref), tri[:LANES, :LANES], last_mask,
                        tuple(_stick_init(2 * ts) for _ in chains))
    nb = kc_ref.shape[1] // blk
    blocks = lambda ref, j: [ref[ch * LANES:(ch + 1) * LANES, pl.ds(pl.multiple_of(j * blk, blk), blk)].astype(BF16)
                             for ch in chains]
    _, state = _stick_scan(
        nb, lambda n, st: _stick_step(qsts, blocks(kc_ref, nb - 1 - n), blocks(vc_ref, nb - 1 - n), tri, None, st,
                                      kv_transposed=True),
        state)
    _stick_finish(state, g_ref, o_ref)


def _sample_attention(qkv_bf, ca_k, ca_v, cbt_k, cbt_v, cb_tail_k, cb_tail_v, dec_batch, dec_seq,
                      diff_lambda, a_g, b_g, lam_init):
    n_past = cbt_k.shape[2]
    main = n_past - N_META
    tail_blk = main // N_META
    full = lambda b, g: (0, 0)

    def new_specs(cq, ck, cv, chains):
        new = lambda off: pl.BlockSpec((dec_seq, chains * LANES), lambda b, g: (b, off // chains + g))
        return [new(cq), new(ck), new(cv)], pl.BlockSpec((dec_seq, chains * LANES), lambda b, g: (b, g))

    w = STICK_CHAINS * LANES
    b_main = pl.BlockSpec((None, w, main), lambda b, g: (b, g, 0))
    b_tail = pl.BlockSpec((None, N_META, w), lambda b, g: (b, 0, g))
    a_main = pl.BlockSpec((None, main * A_HEADS, LANES), lambda b, g: (b, 0, 0))
    a_tail = pl.BlockSpec((None, N_META * A_HEADS, LANES), lambda b, g: (b, tail_blk, 0))
    out_shape = jax.ShapeDtypeStruct((dec_batch * dec_seq, A_WIDTH), BF16)
    sem = ("parallel", "parallel")
    a_new, a_out_spec = new_specs(COL_AQ, COL_AK, COL_AV, DIFF_CHAINS)
    a_out = pl.pallas_call(
        functools.partial(_diff_sample_body, lam_init=lam_init),
        grid=(dec_batch, A_HEADS // DIFF_CHAINS),
        in_specs=(a_new + [a_main, a_main, a_tail, a_tail]
                  + [pl.BlockSpec((4, HEAD_DIM), full), pl.BlockSpec((1, LANES), full)]),
        out_specs=a_out_spec, out_shape=out_shape, scratch_shapes=_softmax_scratch(ATT_BLOCK, 2 * dec_seq),
        compiler_params=_params(sem), name="diff_sample",
    )(qkv_bf, qkv_bf, qkv_bf, ca_k, ca_v, ca_k, ca_v, diff_lambda, a_g.reshape(1, LANES))
    b_g2 = jnp.concatenate([b_g, b_g]).reshape(1, LANES)
    b_new, b_out_spec = new_specs(COL_BQ, COL_BK, COL_BV, STICK_CHAINS)
    b_out = pl.pallas_call(
        _stick_sample_body,
        grid=(dec_batch, B_HEADS // 2 // STICK_CHAINS),
        in_specs=b_new + [b_main, b_main, b_tail, b_tail, pl.BlockSpec((1, LANES), full)],
        out_specs=b_out_spec, out_shape=out_shape,
        compiler_params=_params(sem), name="stick_sample",
    )(qkv_bf, qkv_bf, qkv_bf, cbt_k, cbt_v, cb_tail_k, cb_tail_v, b_g2)
    return a_out, b_out


def _post_body(x_ref, a_ref, b_ref, wo_ref, g_ref, wqt_ref, sk_ref, h_ref, hn_ref, s_ref):
    h = (x_ref[...] + _dot(a_ref[...], wo_ref[:A_WIDTH, :]) + _dot(b_ref[...], wo_ref[A_WIDTH:, :]))
    h_ref[...] = h
    hn = h * lax.rsqrt(jnp.mean(h * h, axis=-1, keepdims=True) + EPS) * g_ref[...]
    hn_ref[...] = hn.astype(BF16)
    hnt = hn.T.astype(BF16)
    half = D_KEY // 2
    for hd in range(PEER_HEADS):
        qt = _dot(wqt_ref[hd * D_KEY:(hd + 1) * D_KEY, :], hnt).astype(BF16)
        for p in range(2):
            s_ref[2 * hd + p] = _dot(sk_ref[p], qt[p * half:(p + 1) * half, :])


def _post_attention(x2d, a_out, b_out, wo_bf, g_ffn, wq_bf, sk_bf, tm):
    n = x2d.shape[0]
    row = lambda i: (i, 0)
    full = lambda i: (0, 0)
    return pl.pallas_call(
        _post_body,
        grid=(n // tm,),
        in_specs=[pl.BlockSpec((tm, D_MODEL), row), pl.BlockSpec((tm, A_WIDTH), row), pl.BlockSpec((tm, B_WIDTH), row),
                  pl.BlockSpec((MIX_WIDTH, D_MODEL), full), pl.BlockSpec((1, D_MODEL), full),
                  pl.BlockSpec((PEER_HEADS * D_KEY, D_MODEL), full),
                  pl.BlockSpec((2, N_KEYS, D_KEY // 2), lambda i: (0, 0, 0))],
        out_specs=[pl.BlockSpec((tm, D_MODEL), row), pl.BlockSpec((tm, D_MODEL), row),
                   pl.BlockSpec((2 * PEER_HEADS, N_KEYS, tm), lambda i: (0, 0, i))],
        out_shape=[jax.ShapeDtypeStruct((n, D_MODEL), F32), jax.ShapeDtypeStruct((n, D_MODEL), BF16),
                   jax.ShapeDtypeStruct((2 * PEER_HEADS, N_KEYS, n), F32)],
        compiler_params=_params(("parallel",)),
        name="post",
    )(x2d, a_out, b_out, wo_bf, g_ffn.reshape(1, D_MODEL), wq_bf, sk_bf)


def _extract_top(s, iota, take, val_ref, aux, aux_ref):
    n_rows = float(s.shape[0])
    for r in range(take):
        m = jnp.max(s, axis=0, keepdims=True)
        pos = jnp.min(jnp.where(s == m, iota, n_rows), axis=0, keepdims=True)
        sel = iota == pos
        val_ref[r:r + 1, :] = m
        if aux is None:
            aux_ref[r:r + 1, :] = pos
        else:
            aux_ref[r:r + 1, :] = jnp.sum(jnp.where(sel, aux, 0.0), axis=0, keepdims=True)
        s = jnp.where(sel, -jnp.inf, s)


def _topk_body(s_ref, row_ref, col_ref, gate_ref, v0_ref, i0_ref, v1_ref, i1_ref, bs_ref, id_ref,
               rows_sc, cols_sc, gates_sc):
    tt = s_ref.shape[2]
    k = PEER_TOPK
    key_iota = lax.broadcasted_iota(jnp.int32, (N_KEYS, tt), 0).astype(F32)
    n_cand = k + 8 * 7 + 8
    cand_iota = lax.broadcasted_iota(jnp.int32, (n_cand, tt), 0).astype(F32)

    def head(h, carry):
        _extract_top(s_ref[2 * h], key_iota, k, v0_ref, None, i0_ref)
        _extract_top(s_ref[2 * h + 1], key_iota, k, v1_ref, None, i1_ref)
        v1_all, i1_all = v1_ref[...], i1_ref[...]
        cs = [v0_ref[0:1, :] + v1_all]
        ci = [i0_ref[0:1, :] * N_KEYS + i1_all]
        for r in range(1, 8):
            cs.append(v0_ref[r:r + 1, :] + v1_all[:8])
            ci.append(i0_ref[r:r + 1, :] * N_KEYS + i1_all[:8])
        cs.append(v0_ref[8:16, :] + v1_all[0:1])
        ci.append(i0_ref[8:16, :] * N_KEYS + i1_all[0:1])
        _extract_top(jnp.concatenate(cs, axis=0), cand_iota, k, bs_ref, jnp.concatenate(ci, axis=0), id_ref)
        best = bs_ref[...]
        e = jnp.exp(best - best[0:1])
        ids = id_ref[...]
        rows = jnp.floor(ids * (1.0 / N_KEYS))
        dst = pl.ds(pl.multiple_of(h * k, k), k)
        gates_sc[dst, :] = e / jnp.sum(e, axis=0, keepdims=True)
        rows_sc[dst, :] = rows
        cols_sc[dst, :] = ids - rows * N_KEYS
        return carry

    lax.fori_loop(0, PEER_HEADS, head, 0, unroll=4)
    row_ref[...] = rows_sc[...].T
    col_ref[...] = cols_sc[...].T
    gate_ref[...] = gates_sc[...].T


def _peer_topk(scores):
    n = scores.shape[2]
    tt = TOPK_TOKENS
    slots = PEER_HEADS * PEER_TOPK
    out = jax.ShapeDtypeStruct((n, slots), F32)
    return pl.pallas_call(
        _topk_body,
        grid=(n // tt,),
        in_specs=[pl.BlockSpec((2 * PEER_HEADS, N_KEYS, tt), lambda i: (0, 0, i))],
        out_specs=[pl.BlockSpec((tt, slots), lambda i: (i, 0))] * 3,
        out_shape=[out] * 3,
        scratch_shapes=[pltpu.VMEM((PEER_TOPK, tt), F32)] * 6 + [pltpu.VMEM((slots, tt), F32)] * 3,
        compiler_params=_params(("parallel",)),
        name="topk",
    )(scores)


def _gelu_x2(x):
    return x * (1.0 + lax.erf(x * (2.0 ** -0.5)))


def _peer_body(hn_ref, row_ref, col_ref, gate_ref, ut_ref, v_ref, h_ref, gf_ref, y_ref, w_sc, acc_sc):
    tb = hn_ref.shape[0]
    c = pl.program_id(1)
    per_chunk = v_ref.shape[0] // N_KEYS

    @pl.when(c == 0)
    def _():
        acc_sc[...] = jnp.zeros_like(acc_sc)
        sub = lax.broadcasted_iota(jnp.int32, (N_KEYS, LANES), 0).astype(F32)

        def build(t, carry):
            r = row_ref[pl.ds(t, 1), :]
            cc = col_ref[pl.ds(t, 1), :]
            g = gate_ref[pl.ds(t, 1), :] * 0.5
            at = jnp.where(sub == r, g, 0.0).astype(BF16)
            bt = jnp.where(sub == cc, 1.0, 0.0).astype(BF16)
            w_sc[pl.ds(pl.multiple_of(t * W_PITCH, 8), N_KEYS), :] = _dot_nt(at, bt)
            return carry

        lax.fori_loop(0, tb, build, 0, unroll=PEER_BUILD_UNROLL)

    x = hn_ref[...]
    total = None
    for s in range(v_ref.shape[0] // PEER_SUB):
        experts = slice(s * PEER_SUB, (s + 1) * PEER_SUB)
        pre = _dot(x, ut_ref[:, experts])
        first = c * per_chunk + s * (PEER_SUB // N_KEYS)
        w = jnp.concatenate([w_sc[pl.ds(first + a, tb, stride=W_PITCH), :] for a in range(PEER_SUB // N_KEYS)],
                            axis=1)
        part = _dot((_gelu_x2(pre) * w).astype(BF16), v_ref[experts, :])
        total = part if total is None else total + part
    acc_sc[...] += total

    @pl.when(c == pl.num_programs(1) - 1)
    def _():
        h = h_ref[...] + acc_sc[...]
        y_ref[...] = h * lax.rsqrt(jnp.mean(h * h, axis=-1, keepdims=True) + EPS) * gf_ref[...]


def _peer_dense(hn, rows, cols, gates, ut_bf, v_bf, h, g_final):
    n = hn.shape[0]
    tb = PEER_TOKENS
    slots = PEER_HEADS * PEER_TOPK
    tok = lambda i, c: (i, 0)
    chunk = lambda i, c: (c, 0)
    return pl.pallas_call(
        _peer_body,
        grid=(n // tb, N_EXPERTS // PEER_CHUNK),
        in_specs=[pl.BlockSpec((tb, D_MODEL), tok), pl.BlockSpec((tb, slots), tok), pl.BlockSpec((tb, slots), tok),
                  pl.BlockSpec((tb, slots), tok), pl.BlockSpec((D_MODEL, PEER_CHUNK), lambda i, c: (0, c)),
                  pl.BlockSpec((PEER_CHUNK, D_MODEL), chunk), pl.BlockSpec((tb, D_MODEL), tok),
                  pl.BlockSpec((1, D_MODEL), lambda i, c: (0, 0))],
        out_specs=pl.BlockSpec((tb, D_MODEL), tok),
        out_shape=jax.ShapeDtypeStruct((n, D_MODEL), F32),
        scratch_shapes=[pltpu.VMEM((tb * W_PITCH, LANES), F32), pltpu.VMEM((tb, D_MODEL), F32)],
        compiler_params=_params(("parallel", "arbitrary")),
        name="peer",
    )(hn, rows, cols, gates, ut_bf, v_bf, h, g_final.reshape(1, D_MODEL))


def _finish(x2d, a_out, b_out, wo_bf, g_ffn, wq_bf, sk_bf, u_bf, v_bf, g_final):
    h, hn, scores = _post_attention(x2d, a_out, b_out, wo_bf, g_ffn, wq_bf, sk_bf, min(POST_TILE, x2d.shape[0]))
    rows, cols, gates = _peer_topk(scores)
    return _peer_dense(hn, rows, cols, gates, u_bf, v_bf, h, g_final)


def kernel(x_prompt, x_sample, cache_a_k, cache_a_v, cache_b_k, cache_b_v, meta_tokens, g_attn, w_qkv,
           diff_lambda, a_norm_g, b_norm_g, w_o, g_ffn, w_peer_q, peer_sub_keys, peer_u, peer_v, g_final):
    batch, seq, _ = x_prompt.shape
    dec_batch, dec_seq, _ = x_sample.shape
    depth = w_qkv.shape[0]
    assert depth == 1, "single-layer step"
    n_past = cache_a_k.shape[2]
    lam_init = 0.8 - 0.6 * math.exp(-0.3 * 0)

    w_bf = w_qkv[0].astype(BF16)
    wo_bf = w_o[0].astype(BF16)
    wq_bf = w_peer_q[0].astype(BF16).T
    sk_bf = peer_sub_keys[0].astype(BF16)
    u_bf = peer_u[0].astype(BF16).T
    v_bf = peer_v[0].astype(BF16)
    g1, g2 = g_attn[0], g_ffn[0]
    lam_p, a_g, b_g = diff_lambda[0], a_norm_g[0], b_norm_g[0]

    xp = x_prompt.reshape(batch * seq, D_MODEL)
    xs = x_sample.reshape(dec_batch * dec_seq, D_MODEL)
    p_ak, p_av, p_bk, p_bv, p_bf = _qkv_project(xp, N_META + jnp.arange(seq, dtype=jnp.int32), g1, w_bf, ROW_TILE,
                                                lead=N_META)
    m_ak, m_av, m_bk, m_bv, m_bf = _qkv_project(meta_tokens.astype(F32), jnp.arange(N_META, dtype=jnp.int32),
                                                g1, w_bf, N_META)
    s_ak, s_av, s_bk, s_bv, s_bf = _qkv_project(xs, n_past + jnp.arange(dec_seq, dtype=jnp.int32), g1, w_bf, dec_seq)

    meta_pad = jnp.pad(m_bf, ((0, LANES - N_META), (0, 0)))
    pa, pb = _prompt_attention(p_bf, meta_pad, batch, seq, lam_p, a_g, b_g, lam_init)
    y_prompt = _finish(xp, pa, pb, wo_bf, g2, wq_bf, sk_bf, u_bf, v_bf, g_final)

    ca_k = cache_a_k[0].reshape(dec_batch, n_past * A_HEADS, 2 * HEAD_DIM)
    ca_v = cache_a_v[0].reshape(dec_batch, n_past * A_HEADS, 2 * HEAD_DIM)
    feature_major = lambda c: jnp.transpose(c[0], (0, 2, 3, 1)).reshape(dec_batch, B_WIDTH, n_past)
    tail_rows = lambda c: c[0, :, n_past - N_META:].reshape(dec_batch, N_META, B_WIDTH)
    sa, sb = _sample_attention(s_bf, ca_k, ca_v, feature_major(cache_b_k), feature_major(cache_b_v),
                               tail_rows(cache_b_k), tail_rows(cache_b_v), dec_batch, dec_seq,
                               lam_p, a_g, b_g, lam_init)
    y_sample = _finish(xs, sa, sb, wo_bf, g2, wq_bf, sk_bf, u_bf, v_bf, g_final)

    def prompt_cache_a(meta_rows, rows):
        full = rows.reshape(batch, (N_META + seq) * A_HEADS, 2 * HEAD_DIM)
        full = lax.dynamic_update_slice(full, jnp.broadcast_to(meta_rows[None], (batch,) + meta_rows.shape), (0, 0, 0))
        return full.reshape(1, batch, N_META + seq, A_HEADS, 2 * HEAD_DIM)

    def prompt_cache_b(meta_rows, frame_rows):
        m = jnp.broadcast_to(meta_rows[None], (batch, N_META, B_WIDTH))
        full = jnp.concatenate([m, frame_rows.reshape(batch, seq, B_WIDTH)], axis=1)
        return full.reshape(1, batch, N_META + seq, B_HEADS, HEAD_DIM)

    sample_cache = lambda rows, heads: rows.reshape(1, dec_batch, dec_seq, heads, A_WIDTH // heads)
    return (y_prompt.reshape(batch, seq, D_MODEL), y_sample.reshape(dec_batch, dec_seq, D_MODEL),
            prompt_cache_a(m_ak, p_ak), prompt_cache_a(m_av, p_av),
            prompt_cache_b(m_bk, p_bk), prompt_cache_b(m_bv, p_bv),
            sample_cache(s_ak, A_HEADS), sample_cache(s_av, A_HEADS),
            sample_cache(s_bk, B_HEADS), sample_cache(s_bv, B_HEADS))
```

```python
import functools
import math

import jax
import jax.numpy as jnp
from jax import lax
from jax.experimental import pallas as pl
from jax.experimental.pallas import tpu as pltpu

F32 = jnp.float32
BF16 = jnp.bfloat16

D_MODEL = 1024
CHUNK = 64
N_META = 16
HEAD_DIM = 64
A_HEADS = 4
B_HEADS = 8
A_WIDTH = A_HEADS * 2 * HEAD_DIM
B_WIDTH = B_HEADS * HEAD_DIM
MIX_WIDTH = A_WIDTH + B_WIDTH
QKV_WIDTH = 3 * MIX_WIDTH
ROT_DIM = HEAD_DIM // 4
ROPE_THETA = 500000.0
N_KEYS = 128
N_EXPERTS = N_KEYS * N_KEYS
PEER_HEADS = 8
PEER_TOPK = 16
D_KEY = 256
EPS = 1e-6
NEG_BIG = -1e30

LANES = 128
ROW_TILE = 256
POST_TILE = 512
ATT_BLOCK = 256
TOPK_TOKENS = 128
PEER_TOKENS = 256
PEER_CHUNK = 2048
PEER_SUB = 2048
PEER_BUILD_UNROLL = 32
W_PITCH = N_KEYS + 8
STICK_CHAINS = 4
DIFF_CHAINS = 1
STICK_EXIT = -110.0
VMEM_LIMIT = 56 * 1024 * 1024

COL_AQ, COL_AK, COL_AV = 0, 4, 8
COL_BQ, COL_BK, COL_BV = 12, 16, 20


def _dot(a, b):
    return jnp.dot(a, b, preferred_element_type=F32)


def _dot_nt(a, b):
    return lax.dot_general(a, b, (((1,), (1,)), ((), ())), preferred_element_type=F32)


def _dot_tn(a, b):
    return lax.dot_general(a, b, (((0,), (0,)), ((), ())), preferred_element_type=F32)


def _params(sem):
    return pltpu.CompilerParams(dimension_semantics=sem, vmem_limit_bytes=VMEM_LIMIT)


def _qkv_body(x_ref, g_ref, w_ref, c_ref, s1_ref, s2_ref, ak_ref, av_ref, bk_ref, bv_ref, bf_ref, *, b_feature_major):
    x = x_ref[...]
    xn = (x * lax.rsqrt(jnp.mean(x * x, axis=-1, keepdims=True) + EPS) * g_ref[...]).astype(BF16)
    cos_t = c_ref[...]
    sin_lo = s1_ref[...]
    sin_hi = s2_ref[...]
    f32_outs = {1: ak_ref, 2: av_ref, 4: bk_ref, 5: bv_ref}
    for grp in range(6):
        for half in range(2):
            y2 = _dot(xn, w_ref[:, (grp * 4 + half * 2) * LANES:(grp * 4 + half * 2 + 2) * LANES])
            for k in range(2):
                j = half * 2 + k
                cb = grp * 4 + j
                y = y2[:, k * LANES:(k + 1) * LANES]
                if grp in (0, 1):
                    y = (y * cos_t + pltpu.roll(y, LANES - ROT_DIM // 2, 1) * sin_lo
                         + pltpu.roll(y, ROT_DIM // 2, 1) * sin_hi)
                if grp in (1, 2):
                    f32_outs[grp][pl.ds(j, x.shape[0], stride=A_HEADS), :] = y
                elif grp in f32_outs and b_feature_major:
                    f32_outs[grp][j * LANES:(j + 1) * LANES, :] = y.T
                elif grp in f32_outs:
                    f32_outs[grp][:, j * LANES:(j + 1) * LANES] = y
                if grp in (0, 3):
                    y = y * (HEAD_DIM ** -0.5)
                bf_ref[:, cb * LANES:(cb + 1) * LANES] = y.astype(BF16)


def _rope_tables(pos):
    half = ROT_DIM // 2
    inv_freq = ROPE_THETA ** (-jnp.arange(half, dtype=F32) * 2.0 / ROT_DIM)
    ang = pos.astype(F32)[:, None] * inv_freq[None, :]
    cos, sin = jnp.cos(ang), jnp.sin(ang)
    t = pos.shape[0]
    pad = jnp.zeros((t, HEAD_DIM - ROT_DIM), F32)
    zero = jnp.zeros((t, half), F32)
    cos_t = jnp.concatenate([cos, cos, pad + 1.0], axis=1)
    sin_lo = jnp.concatenate([-sin, zero, pad], axis=1)
    sin_hi = jnp.concatenate([zero, sin, pad], axis=1)
    tile = lambda a: jnp.concatenate([a, a], axis=1)
    return tile(cos_t), tile(sin_lo), tile(sin_hi)


def _qkv_project(x2d, pos, g, w_bf, tm, lead=0, b_feature_major=False):
    n = x2d.shape[0]
    t = pos.shape[0]
    per = t // tm
    tabs = _rope_tables(pos)
    row = lambda b, j: (b * per + j, 0)
    tab = lambda b, j: (j, 0)
    full = lambda b, j: (0, 0)
    f32_out = jax.ShapeDtypeStruct((n, A_WIDTH), F32)
    a_rows = tm * A_HEADS
    a_out = jax.ShapeDtypeStruct(((n // t) * (lead + t) * A_HEADS, LANES), F32)
    a_spec = pl.BlockSpec((pl.Element(a_rows), pl.Element(LANES)),
                          lambda b, j: (pl.multiple_of(b * ((lead + t) * A_HEADS) + lead * A_HEADS + j * a_rows, 8),
                                        0))
    if b_feature_major:
        b_out = jax.ShapeDtypeStruct((n // t, B_WIDTH, t), F32)
        b_spec = pl.BlockSpec((None, B_WIDTH, tm), lambda b, j: (b, 0, j))
    else:
        b_out, b_spec = f32_out, pl.BlockSpec((tm, B_WIDTH), row)
    return pl.pallas_call(
        functools.partial(_qkv_body, b_feature_major=b_feature_major),
        grid=(n // t, per),
        in_specs=[pl.BlockSpec((tm, D_MODEL), row), pl.BlockSpec((1, D_MODEL), full),
                  pl.BlockSpec((D_MODEL, QKV_WIDTH), full),
                  pl.BlockSpec((tm, LANES), tab), pl.BlockSpec((tm, LANES), tab), pl.BlockSpec((tm, LANES), tab)],
        out_specs=[a_spec, a_spec, b_spec, b_spec, pl.BlockSpec((tm, QKV_WIDTH), row)],
        out_shape=[a_out, a_out, b_out, b_out, jax.ShapeDtypeStruct((n, QKV_WIDTH), BF16)],
        compiler_params=_params(("parallel", "parallel")),
        name="qkv",
    )(x2d, g.reshape(1, D_MODEL), w_bf, *tabs)


def _stack_heads_t(q):
    qt = q.astype(F32).T
    sub = lax.broadcasted_iota(jnp.int32, qt.shape, 0)
    first = jnp.where(sub < HEAD_DIM, qt, 0.0)
    second = jnp.where(sub >= HEAD_DIM, qt, 0.0)
    return jnp.concatenate([first, second], axis=1).astype(BF16)


def _softmax_init(n2):
    return jnp.full((1, n2), NEG_BIG, F32), jnp.zeros((1, n2), F32)


def _softmax_update(s, m, l):
    m_new = jnp.maximum(m, jnp.max(s, axis=0, keepdims=True))
    alpha = jnp.exp(m - m_new)
    p = jnp.exp(s - m_new)
    return m_new, alpha, alpha * l + jnp.sum(p, axis=0, keepdims=True), p.astype(BF16)


def _softmax_first(kbs, vbs, mask, chains):
    state = []
    for kb, vb, (qst_sc, _, _, acc_sc) in zip(kbs, vbs, chains):
        m, l = _softmax_init(qst_sc.shape[1])
        sc = jnp.where(mask, _dot(kb, qst_sc[...]), NEG_BIG)
        m, _, l, p = _softmax_update(sc, m, l)
        acc_sc[...] = _dot_tn(vb, p)
        state.append((m, l))
    return tuple(state)


def _softmax_scratch(tk, n2):
    return [pltpu.VMEM((LANES, n2), BF16), pltpu.VMEM((tk, n2), F32), pltpu.VMEM((tk, n2), BF16),
            pltpu.VMEM((LANES, n2), F32)] * DIFF_CHAINS


def _chain_scratch(scratch):
    return [scratch[4 * ch:4 * ch + 4] for ch in range(len(scratch) // 4)]


def _softmax_scan(n_blocks, kb_fn, vb_fn, last_mask, state, chains):
    n2 = chains[0][0].shape[1]
    mm_cols = min(2 * LANES, n2)

    def block_step(state, v_prevs, k_nexts, mask):
        stats = [([], []) for _ in chains]
        for c0 in range(0, n2, mm_cols):
            wide = slice(c0, c0 + mm_cols)
            pvs = [_dot_tn(v_prev, p_sc[:, wide]) for v_prev, (_, _, p_sc, _) in zip(v_prevs, chains)]
            tiles = []
            for ch, (_, s_sc, _, _) in enumerate(chains):
                for t0 in range(c0, c0 + mm_cols, LANES):
                    cols = slice(t0, t0 + LANES)
                    sc = s_sc[:, cols]
                    if mask is not None:
                        sc = jnp.where(mask[:, cols], sc, NEG_BIG)
                    tiles.append((ch, cols, sc))
            if k_nexts is not None:
                for k_next, (qst_sc, s_sc, _, _) in zip(k_nexts, chains):
                    s_sc[:, wide] = _dot(k_next, qst_sc[:, wide])
            for ch, cols, sc in tiles:
                _, _, p_sc, acc_sc = chains[ch]
                m, l = state[ch]
                m_new, alpha, l_new, p = _softmax_update(sc, m[:, cols], l[:, cols])
                p_sc[:, cols] = p
                off = cols.start - c0
                acc_sc[:, cols] = alpha * (acc_sc[:, cols] + pvs[ch][:, off:off + LANES])
                stats[ch][0].append(m_new)
                stats[ch][1].append(l_new)
        return tuple((jnp.concatenate(ms, axis=1), jnp.concatenate(ls, axis=1)) for ms, ls in stats)

    for kb, (qst_sc, s_sc, p_sc, _) in zip(kb_fn(0), chains):
        s_sc[...] = _dot(kb, qst_sc[...])
        p_sc[...] = jnp.zeros(p_sc.shape, BF16)
    last = n_blocks - 1

    def body(j, carry):
        return block_step(carry, vb_fn(jnp.maximum(j - 1, 0)), kb_fn(j + 1), None)

    state = lax.fori_loop(0, last // 2, lambda jj, carry: body(2 * jj + 1, body(2 * jj, carry)), state)
    state = lax.cond(last % 2 == 1, lambda carry: body(last - 1, carry), lambda carry: carry, state)
    state = block_step(state, vb_fn(jnp.maximum(last - 1, 0)), None, last_mask)
    for vb, (_, _, p_sc, acc_sc) in zip(vb_fn(last), chains):
        acc_sc[...] += _dot_tn(vb, p_sc[...])
    return state


def _diff_finish(state, chains, lam_ref, g_ref, o_ref, lam_init):
    for ch, ((_, l), (_, _, _, acc_sc)) in enumerate(zip(state, chains)):
        _diff_finish_one(l, acc_sc, lam_ref, g_ref, o_ref.at[:, ch * LANES:(ch + 1) * LANES], lam_init)


def _diff_finish_one(l, acc_sc, lam_ref, g_ref, o_ref, lam_init):
    tq = o_ref.shape[0]
    acc = acc_sc[...]
    lp = lam_ref[...]
    lam = (jnp.exp(jnp.sum(lp[0:1] * lp[1:2], axis=1, keepdims=True))
           - jnp.exp(jnp.sum(lp[2:3] * lp[3:4], axis=1, keepdims=True)) + lam_init)
    on = (acc / l).T
    o = on[:tq] - lam * on[tq:]
    o = o * lax.rsqrt(jnp.mean(o * o, axis=1, keepdims=True) + EPS) * g_ref[...] * (1.0 - lam_init)
    o_ref[...] = o.astype(o_ref.dtype)


def _stick_init(n2):
    return jnp.zeros((1, n2), F32), jnp.zeros((LANES, n2), F32)


def _stick_step(qsts, kbs, vbs, tri, mask_fn, state, kv_transposed=False):
    zs = [(_dot_tn(kb, q) if kv_transposed else _dot(kb, q)) for q, kb in zip(qsts, kbs)]
    mask = None
    if mask_fn is not None:
        mask = mask_fn(lax.broadcasted_iota(jnp.int32, zs[0].shape, 0),
                       lax.broadcasted_iota(jnp.int32, zs[0].shape, 1))
    cums = []
    for z in zs:
        lm = -(jnp.maximum(z, 0.0) + jnp.log(1.0 + jnp.exp(-jnp.abs(z))))
        if mask is not None:
            lm = jnp.where(mask, lm, 0.0)
        hi = lm.astype(BF16)
        lo = (lm - hi.astype(F32)).astype(BF16)
        cums.append(_dot(tri, hi) + _dot(tri, lo))
    out = []
    for z, cum, vb, (c, acc) in zip(zs, cums, vbs, state):
        a = jnp.exp(z + cum + c)
        if mask is not None:
            a = jnp.where(mask, a, 0.0)
        a = a.astype(BF16)
        out.append((c + cum[0:1, :], acc + (_dot(vb, a) if kv_transposed else _dot_tn(vb, a))))
    return tuple(out)


def _stick_cmax(state):
    return functools.reduce(jnp.maximum, [jnp.max(c) for c, _ in state])


def _tri(n):
    r = lax.broadcasted_iota(jnp.int32, (n, n), 0)
    c = lax.broadcasted_iota(jnp.int32, (n, n), 1)
    return jnp.where(c >= r, 1.0, 0.0).astype(BF16)


def _stick_finish(state, g_ref, o_ref):
    tq = o_ref.shape[0]
    lane = lax.broadcasted_iota(jnp.int32, (tq, LANES), 1)
    first = lane < HEAD_DIM
    for ch, (_, acc) in enumerate(state):
        acc = acc.T
        o = jnp.where(first, acc[:tq], acc[tq:])
        sq = o * o
        ss0 = jnp.sum(jnp.where(first, sq, 0.0), axis=1, keepdims=True)
        ss1 = jnp.sum(jnp.where(first, 0.0, sq), axis=1, keepdims=True)
        ms = jnp.where(first, ss0, ss1) * (1.0 / HEAD_DIM)
        o_ref[:, ch * LANES:(ch + 1) * LANES] = (o * lax.rsqrt(ms + EPS) * g_ref[...]).astype(o_ref.dtype)


def _stick_scan(n_blocks, step_fn, state):
    def cond(st):
        return jnp.logical_and(st[0] < n_blocks, st[1] > STICK_EXIT)

    def body(st):
        state = step_fn(st[0], st[2])
        return st[0] + 1, _stick_cmax(state), state

    out = lax.while_loop(cond, body, (jnp.int32(0), _stick_cmax(state), state))
    return out[1], out[2]


def _diff_prompt_body(q_ref, k_ref, v_ref, km_ref, vm_ref, lam_ref, g_ref, o_ref, *scratch, lam_init):
    tq = q_ref.shape[0]
    i = pl.program_id(2)
    chains = _chain_scratch(scratch)
    heads = range(len(chains))
    part = lambda x, ch: x[:, ch * LANES:(ch + 1) * LANES]
    for ch in heads:
        chains[ch][0][...] = _stack_heads_t(part(q_ref[...], ch))
    nm = km_ref.shape[0]
    key = lax.broadcasted_iota(jnp.int32, (nm, 2 * tq), 0)
    meta = lambda ref: [part(ref[...], ch) for ch in heads]
    state = _softmax_first(meta(km_ref), meta(vm_ref), key < N_META, chains)
    key = lax.broadcasted_iota(jnp.int32, (tq, 2 * tq), 0)
    query = lax.broadcasted_iota(jnp.int32, (tq, 2 * tq), 1) % tq
    block = lambda ref: (lambda j: [ref[pl.ds(pl.multiple_of(j * tq, tq), tq), ch * LANES:(ch + 1) * LANES]
                                    for ch in heads])
    state = _softmax_scan(i + 1, block(k_ref), block(v_ref), (key // CHUNK) <= (query // CHUNK), state, chains)
    _diff_finish(state, chains, lam_ref, g_ref, o_ref, lam_init)


def _stick_prompt_body(q_ref, k_ref, v_ref, km_ref, vm_ref, g_ref, o_ref):
    tq = q_ref.shape[0]
    i = pl.program_id(2)
    chains = range(q_ref.shape[1] // LANES)
    part = lambda x, ch: x[:, ch * LANES:(ch + 1) * LANES]
    qsts = [_stack_heads_t(part(q_ref[...], ch)) for ch in chains]
    tri = _tri(tq)
    blocks = lambda ref, j: [ref[pl.ds(pl.multiple_of(j * tq, tq), tq), ch * LANES:(ch + 1) * LANES] for ch in chains]
    state = _stick_step(qsts, blocks(k_ref, i), blocks(v_ref, i), tri, lambda key, col: key < col % tq,
                        tuple(_stick_init(2 * tq) for _ in chains))
    cmax, state = _stick_scan(
        i, lambda n, st: _stick_step(qsts, blocks(k_ref, i - 1 - n), blocks(v_ref, i - 1 - n), tri, None, st), state)
    nm = km_ref.shape[0]
    meta = lambda ref: [part(ref[...], ch) for ch in chains]
    state = lax.cond(cmax > STICK_EXIT,
                     lambda st: _stick_step(qsts, meta(km_ref), meta(vm_ref), tri[:nm, :nm],
                                            lambda key, col: key < N_META, st),
                     lambda st: st, state)
    _stick_finish(state, g_ref, o_ref)


def _prompt_attention(qkv_bf, meta_bf, batch, seq, diff_lambda, a_g, b_g, lam_init):
    tq = ATT_BLOCK
    nq = seq // tq
    n = batch * seq
    full = lambda b, g, i: (0, 0)
    nm = meta_bf.shape[0]

    def specs(cq, ck, cv, chains):
        w = chains * LANES
        col = lambda off: off // chains
        return ([pl.BlockSpec((tq, w), lambda b, g, i: (b * nq + i, col(cq) + g)),
                 pl.BlockSpec((seq, w), lambda b, g, i: (b, col(ck) + g)),
                 pl.BlockSpec((seq, w), lambda b, g, i: (b, col(cv) + g)),
                 pl.BlockSpec((nm, w), lambda b, g, i: (0, col(ck) + g)),
                 pl.BlockSpec((nm, w), lambda b, g, i: (0, col(cv) + g))],
                pl.BlockSpec((tq, w), lambda b, g, i: (b * nq + i, g)))

    out_shape = jax.ShapeDtypeStruct((n, A_WIDTH), BF16)
    sem = ("parallel", "parallel", "arbitrary")
    a_in, a_out_spec = specs(COL_AQ, COL_AK, COL_AV, DIFF_CHAINS)
    a_out = pl.pallas_call(
        functools.partial(_diff_prompt_body, lam_init=lam_init),
        grid=(batch, A_HEADS // DIFF_CHAINS, nq),
        in_specs=a_in + [pl.BlockSpec((4, HEAD_DIM), full), pl.BlockSpec((1, LANES), full)],
        out_specs=a_out_spec, out_shape=out_shape, scratch_shapes=_softmax_scratch(tq, 2 * tq),
        compiler_params=_params(sem), name="diff_prompt",
    )(qkv_bf, qkv_bf, qkv_bf, meta_bf, meta_bf, diff_lambda, a_g.reshape(1, LANES))
    b_g2 = jnp.concatenate([b_g, b_g]).reshape(1, LANES)
    b_in, b_out_spec = specs(COL_BQ, COL_BK, COL_BV, STICK_CHAINS)
    b_out = pl.pallas_call(
        _stick_prompt_body,
        grid=(batch, B_HEADS // 2 // STICK_CHAINS, nq),
        in_specs=b_in + [pl.BlockSpec((1, LANES), full)],
        out_specs=b_out_spec, out_shape=out_shape,
        compiler_params=_params(sem), name="stick_prompt",
    )(qkv_bf, qkv_bf, qkv_bf, meta_bf, meta_bf, b_g2)
    return a_out, b_out


def _last_block(tail_ref, new_ref):
    tail = tail_ref[...].astype(BF16)
    new = new_ref[...]
    pad = jnp.zeros((LANES - tail.shape[0] - new.shape[0], LANES), BF16)
    return jnp.concatenate([tail, new, pad], axis=0)


def _diff_sample_body(q_ref, kn_ref, vn_ref, kc_ref, vc_ref, kt_ref, vt_ref, lam_ref, g_ref, o_ref, *scratch,
                      lam_init):
    ts = q_ref.shape[0]
    chains = _chain_scratch(scratch)
    heads = range(len(chains))
    h0 = pl.program_id(1) * len(chains)
    nt = kt_ref.shape[0] // A_HEADS
    part = lambda x, ch: x[:, ch * LANES:(ch + 1) * LANES]
    for ch in heads:
        chains[ch][0][...] = _stack_heads_t(part(q_ref[...], ch))
    key = lax.broadcasted_iota(jnp.int32, (LANES, 2 * ts), 0)
    head_rows = lambda ref, ch, first, n: ref[pl.ds(first * A_HEADS + h0 + ch, n, stride=A_HEADS), :].astype(BF16)
    pad = jnp.zeros((LANES - nt - ts, LANES), BF16)
    last = lambda tail_ref, new_ref: [
        jnp.concatenate([head_rows(tail_ref, ch, 0, nt), part(new_ref[...], ch), pad], axis=0) for ch in heads]
    state = _softmax_first(last(kt_ref, kn_ref), last(vt_ref, vn_ref), key < nt + ts, chains)
    blk = ATT_BLOCK
    block = lambda ref: (lambda j: [head_rows(ref, ch, j * blk, blk) for ch in heads])
    state = _softmax_scan(kc_ref.shape[0] // (A_HEADS * blk), block(kc_ref), block(vc_ref), None, state, chains)
    _diff_finish(state, chains, lam_ref, g_ref, o_ref, lam_init)


def _stick_sample_body(q_ref, kn_ref, vn_ref, kc_ref, vc_ref, kt_ref, vt_ref, g_ref, o_ref):
    ts = q_ref.shape[0]
    nt = kt_ref.shape[0]
    chains = range(q_ref.shape[1] // LANES)
    part = lambda x, ch: x[:, ch * LANES:(ch + 1) * LANES]
    qsts = [_stack_heads_t(part(q_ref[...], ch)) for ch in chains]
    blk = ATT_BLOCK
    tri = _tri(blk)
    pad = jnp.zeros((LANES - nt - ts, LANES), BF16)
    last = lambda tail_ref, new_ref: [
        jnp.concatenate([part(tail_ref[...], ch).astype(BF16), part(new_ref[...], ch), pad], axis=0) for ch in chains]
    last_mask = lambda key, col: (key < nt) | ((key < nt + ts) & (key - nt < col % ts))
    state = _stick_step(qsts, last(kt_ref, kn_ref), last(vt_ref, vn_ref), tri[:LANES, :LANES], last_mask,
                        tuple(_stick_init(2 * ts) for _ in chains))
    nb = kc_ref.shape[1] // blk
    blocks = lambda ref, j: [ref[ch * LANES:(ch + 1) * LANES, pl.ds(pl.multiple_of(j * blk, blk), blk)].astype(BF16)
                             for ch in chains]
    _, state = _stick_scan(
        nb, lambda n, st: _stick_step(qsts, blocks(kc_ref, nb - 1 - n), blocks(vc_ref, nb - 1 - n), tri, None, st,
                                      kv_transposed=True),
        state)
    _stick_finish(state, g_ref, o_ref)


def _sample_attention(qkv_bf, ca_k, ca_v, cbt_k, cbt_v, cb_tail_k, cb_tail_v, dec_batch, dec_seq,
                      diff_lambda, a_g, b_g, lam_init):
    n_past = cbt_k.shape[2]
    main = n_past - N_META
    tail_blk = main // N_META
    full = lambda b, g: (0, 0)

    def new_specs(cq, ck, cv, chains):
        new = lambda off: pl.BlockSpec((dec_seq, chains * LANES), lambda b, g: (b, off // chains + g))
        return [new(cq), new(ck), new(cv)], pl.BlockSpec((dec_seq, chains * LANES), lambda b, g: (b, g))

    w = STICK_CHAINS * LANES
    b_main = pl.BlockSpec((None, w, main), lambda b, g: (b, g, 0))
    b_tail = pl.BlockSpec((None, N_META, w), lambda b, g: (b, 0, g))
    a_main = pl.BlockSpec((None, main * A_HEADS, LANES), lambda b, g: (b, 0, 0))
    a_tail = pl.BlockSpec((None, N_META * A_HEADS, LANES), lambda b, g: (b, tail_blk, 0))
    out_shape = jax.ShapeDtypeStruct((dec_batch * dec_seq, A_WIDTH), BF16)
    sem = ("parallel", "parallel")
    a_new, a_out_spec = new_specs(COL_AQ, COL_AK, COL_AV, DIFF_CHAINS)
    a_out = pl.pallas_call(
        functools.partial(_diff_sample_body, lam_init=lam_init),
        grid=(dec_batch, A_HEADS // DIFF_CHAINS),
        in_specs=(a_new + [a_main, a_main, a_tail, a_tail]
                  + [pl.BlockSpec((4, HEAD_DIM), full), pl.BlockSpec((1, LANES), full)]),
        out_specs=a_out_spec, out_shape=out_shape, scratch_shapes=_softmax_scratch(ATT_BLOCK, 2 * dec_seq),
        compiler_params=_params(sem), name="diff_sample",
    )(qkv_bf, qkv_bf, qkv_bf, ca_k, ca_v, ca_k, ca_v, diff_lambda, a_g.reshape(1, LANES))
    b_g2 = jnp.concatenate([b_g, b_g]).reshape(1, LANES)
    b_new, b_out_spec = new_specs(COL_BQ, COL_BK, COL_BV, STICK_CHAINS)
    b_out = pl.pallas_call(
        _stick_sample_body,
        grid=(dec_batch, B_HEADS // 2 // STICK_CHAINS),
        in_specs=b_new + [b_main, b_main, b_tail, b_tail, pl.BlockSpec((1, LANES), full)],
        out_specs=b_out_spec, out_shape=out_shape,
        compiler_params=_params(sem), name="stick_sample",
    )(qkv_bf, qkv_bf, qkv_bf, cbt_k, cbt_v, cb_tail_k, cb_tail_v, b_g2)
    return a_out, b_out


def _post_body(x_ref, a_ref, b_ref, wo_ref, g_ref, wqt_ref, sk_ref, h_ref, hn_ref, s_ref):
    h = (x_ref[...] + _dot(a_ref[...], wo_ref[:A_WIDTH, :]) + _dot(b_ref[...], wo_ref[A_WIDTH:, :]))
    h_ref[...] = h
    hn = h * lax.rsqrt(jnp.mean(h * h, axis=-1, keepdims=True) + EPS) * g_ref[...]
    hn_ref[...] = hn.astype(BF16)
    hnt = hn.T.astype(BF16)
    half = D_KEY // 2
    for hd in range(PEER_HEADS):
        qt = _dot(wqt_ref[hd * D_KEY:(hd + 1) * D_KEY, :], hnt).astype(BF16)
        for p in range(2):
            s_ref[2 * hd + p] = _dot(sk_ref[p], qt[p * half:(p + 1) * half, :])


def _post_attention(x2d, a_out, b_out, wo_bf, g_ffn, wq_bf, sk_bf, tm):
    n = x2d.shape[0]
    row = lambda i: (i, 0)
    full = lambda i: (0, 0)
    return pl.pallas_call(
        _post_body,
        grid=(n // tm,),
        in_specs=[pl.BlockSpec((tm, D_MODEL), row), pl.BlockSpec((tm, A_WIDTH), row), pl.BlockSpec((tm, B_WIDTH), row),
                  pl.BlockSpec((MIX_WIDTH, D_MODEL), full), pl.BlockSpec((1, D_MODEL), full),
                  pl.BlockSpec((PEER_HEADS * D_KEY, D_MODEL), full),
                  pl.BlockSpec((2, N_KEYS, D_KEY // 2), lambda i: (0, 0, 0))],
        out_specs=[pl.BlockSpec((tm, D_MODEL), row), pl.BlockSpec((tm, D_MODEL), row),
                   pl.BlockSpec((2 * PEER_HEADS, N_KEYS, tm), lambda i: (0, 0, i))],
        out_shape=[jax.ShapeDtypeStruct((n, D_MODEL), F32), jax.ShapeDtypeStruct((n, D_MODEL), BF16),
                   jax.ShapeDtypeStruct((2 * PEER_HEADS, N_KEYS, n), F32)],
        compiler_params=_params(("parallel",)),
        name="post",
    )(x2d, a_out, b_out, wo_bf, g_ffn.reshape(1, D_MODEL), wq_bf, sk_bf)


def _extract_top(s, iota, take, val_ref, aux, aux_ref):
    n_rows = float(s.shape[0])
    for r in range(take):
        m = jnp.max(s, axis=0, keepdims=True)
        pos = jnp.min(jnp.where(s == m, iota, n_rows), axis=0, keepdims=True)
        sel = iota == pos
        val_ref[r:r + 1, :] = m
        if aux is None:
            aux_ref[r:r + 1, :] = pos
        else:
            aux_ref[r:r + 1, :] = jnp.sum(jnp.where(sel, aux, 0.0), axis=0, keepdims=True)
        s = jnp.where(sel, -jnp.inf, s)


def _topk_body(s_ref, row_ref, col_ref, gate_ref, v0_ref, i0_ref, v1_ref, i1_ref, bs_ref, id_ref,
               rows_sc, cols_sc, gates_sc):
    tt = s_ref.shape[2]
    k = PEER_TOPK
    key_iota = lax.broadcasted_iota(jnp.int32, (N_KEYS, tt), 0).astype(F32)
    n_cand = k + 8 * 7 + 8
    cand_iota = lax.broadcasted_iota(jnp.int32, (n_cand, tt), 0).astype(F32)

    def head(h, carry):
        _extract_top(s_ref[2 * h], key_iota, k, v0_ref, None, i0_ref)
        _extract_top(s_ref[2 * h + 1], key_iota, k, v1_ref, None, i1_ref)
        v1_all, i1_all = v1_ref[...], i1_ref[...]
        cs = [v0_ref[0:1, :] + v1_all]
        ci = [i0_ref[0:1, :] * N_KEYS + i1_all]
        for r in range(1, 8):
            cs.append(v0_ref[r:r + 1, :] + v1_all[:8])
            ci.append(i0_ref[r:r + 1, :] * N_KEYS + i1_all[:8])
        cs.append(v0_ref[8:16, :] + v1_all[0:1])
        ci.append(i0_ref[8:16, :] * N_KEYS + i1_all[0:1])
        _extract_top(jnp.concatenate(cs, axis=0), cand_iota, k, bs_ref, jnp.concatenate(ci, axis=0), id_ref)
        best = bs_ref[...]
        e = jnp.exp(best - best[0:1])
        ids = id_ref[...]
        rows = jnp.floor(ids * (1.0 / N_KEYS))
        dst = pl.ds(pl.multiple_of(h * k, k), k)
        gates_sc[dst, :] = e / jnp.sum(e, axis=0, keepdims=True)
        rows_sc[dst, :] = rows
        cols_sc[dst, :] = ids - rows * N_KEYS
        return carry

    lax.fori_loop(0, PEER_HEADS, head, 0, unroll=4)
    row_ref[...] = rows_sc[...].T
    col_ref[...] = cols_sc[...].T
    gate_ref[...] = gates_sc[...].T


def _peer_topk(scores):
    n = scores.shape[2]
    tt = TOPK_TOKENS
    slots = PEER_HEADS * PEER_TOPK
    out = jax.ShapeDtypeStruct((n, slots), F32)
    return pl.pallas_call(
        _topk_body,
        grid=(n // tt,),
        in_specs=[pl.BlockSpec((2 * PEER_HEADS, N_KEYS, tt), lambda i: (0, 0, i))],
        out_specs=[pl.BlockSpec((tt, slots), lambda i: (i, 0))] * 3,
        out_shape=[out] * 3,
        scratch_shapes=[pltpu.VMEM((PEER_TOPK, tt), F32)] * 6 + [pltpu.VMEM((slots, tt), F32)] * 3,
        compiler_params=_params(("parallel",)),
        name="topk",
    )(scores)


def _gelu_x2(x):
    return x * (1.0 + lax.erf(x * (2.0 ** -0.5)))


def _peer_body(hn_ref, row_ref, col_ref, gate_ref, ut_ref, v_ref, h_ref, gf_ref, y_ref, w_sc, acc_sc):
    tb = hn_ref.shape[0]
    c = pl.program_id(1)
    per_chunk = v_ref.shape[0] // N_KEYS

    @pl.when(c == 0)
    def _():
        acc_sc[...] = jnp.zeros_like(acc_sc)
        sub = lax.broadcasted_iota(jnp.int32, (N_KEYS, LANES), 0).astype(F32)

        def build(t, carry):
            r = row_ref[pl.ds(t, 1), :]
            cc = col_ref[pl.ds(t, 1), :]
            g = gate_ref[pl.ds(t, 1), :] * 0.5
            at = jnp.where(sub == r, g, 0.0).astype(BF16)
            bt = jnp.where(sub == cc, 1.0, 0.0).astype(BF16)
            w_sc[pl.ds(pl.multiple_of(t * W_PITCH, 8), N_KEYS), :] = _dot_nt(at, bt)
            return carry

        lax.fori_loop(0, tb, build, 0, unroll=PEER_BUILD_UNROLL)

    x = hn_ref[...]
    total = None
    for s in range(v_ref.shape[0] // PEER_SUB):
        experts = slice(s * PEER_SUB, (s + 1) * PEER_SUB)
        pre = _dot(x, ut_ref[:, experts])
        first = c * per_chunk + s * (PEER_SUB // N_KEYS)
        w = jnp.concatenate([w_sc[pl.ds(first + a, tb, stride=W_PITCH), :] for a in range(PEER_SUB // N_KEYS)],
                            axis=1)
        part = _dot((_gelu_x2(pre) * w).astype(BF16), v_ref[experts, :])
        total = part if total is None else total + part
    acc_sc[...] += total

    @pl.when(c == pl.num_programs(1) - 1)
    def _():
        h = h_ref[...] + acc_sc[...]
        y_ref[...] = h * lax.rsqrt(jnp.mean(h * h, axis=-1, keepdims=True) + EPS) * gf_ref[...]


def _peer_dense(hn, rows, cols, gates, ut_bf, v_bf, h, g_final):
    n = hn.shape[0]
    tb = PEER_TOKENS
    slots = PEER_HEADS * PEER_TOPK
    tok = lambda i, c: (i, 0)
    chunk = lambda i, c: (c, 0)
    return pl.pallas_call(
        _peer_body,
        grid=(n // tb, N_EXPERTS // PEER_CHUNK),
        in_specs=[pl.BlockSpec((tb, D_MODEL), tok), pl.BlockSpec((tb, slots), tok), pl.BlockSpec((tb, slots), tok),
                  pl.BlockSpec((tb, slots), tok), pl.BlockSpec((D_MODEL, PEER_CHUNK), lambda i, c: (0, c)),
                  pl.BlockSpec((PEER_CHUNK, D_MODEL), chunk), pl.BlockSpec((tb, D_MODEL), tok),
                  pl.BlockSpec((1, D_MODEL), lambda i, c: (0, 0))],
        out_specs=pl.BlockSpec((tb, D_MODEL), tok),
        out_shape=jax.ShapeDtypeStruct((n, D_MODEL), F32),
        scratch_shapes=[pltpu.VMEM((tb * W_PITCH, LANES), F32), pltpu.VMEM((tb, D_MODEL), F32)],
        compiler_params=_params(("parallel", "arbitrary")),
        name="peer",
    )(hn, rows, cols, gates, ut_bf, v_bf, h, g_final.reshape(1, D_MODEL))


def _finish(x2d, a_out, b_out, wo_bf, g_ffn, wq_bf, sk_bf, u_bf, v_bf, g_final):
    h, hn, scores = _post_attention(x2d, a_out, b_out, wo_bf, g_ffn, wq_bf, sk_bf, min(POST_TILE, x2d.shape[0]))
    rows, cols, gates = _peer_topk(scores)
    return _peer_dense(hn, rows, cols, gates, u_bf, v_bf, h, g_final)


def kernel(x_prompt, x_sample, cache_a_k, cache_a_v, cache_b_k, cache_b_v, meta_tokens, g_attn, w_qkv,
           diff_lambda, a_norm_g, b_norm_g, w_o, g_ffn, w_peer_q, peer_sub_keys, peer_u, peer_v, g_final):
    batch, seq, _ = x_prompt.shape
    dec_batch, dec_seq, _ = x_sample.shape
    depth = w_qkv.shape[0]
    assert depth == 1, "single-layer step"
    n_past = cache_a_k.shape[2]
    lam_init = 0.8 - 0.6 * math.exp(-0.3 * 0)

    w_bf = w_qkv[0].astype(BF16)
    wo_bf = w_o[0].astype(BF16)
    wq_bf = w_peer_q[0].astype(BF16).T
    sk_bf = peer_sub_keys[0].astype(BF16)
    u_bf = peer_u[0].astype(BF16).T
    v_bf = peer_v[0].astype(BF16)
    g1, g2 = g_attn[0], g_ffn[0]
    lam_p, a_g, b_g = diff_lambda[0], a_norm_g[0], b_norm_g[0]

    xp = x_prompt.reshape(batch * seq, D_MODEL)
    xs = x_sample.reshape(dec_batch * dec_seq, D_MODEL)
    p_ak, p_av, p_bk, p_bv, p_bf = _qkv_project(xp, N_META + jnp.arange(seq, dtype=jnp.int32), g1, w_bf, ROW_TILE,
                                                lead=N_META, b_feature_major=True)
    m_ak, m_av, m_bk, m_bv, m_bf = _qkv_project(meta_tokens.astype(F32), jnp.arange(N_META, dtype=jnp.int32),
                                                g1, w_bf, N_META)
    s_ak, s_av, s_bk, s_bv, s_bf = _qkv_project(xs, n_past + jnp.arange(dec_seq, dtype=jnp.int32), g1, w_bf, dec_seq)

    meta_pad = jnp.pad(m_bf, ((0, LANES - N_META), (0, 0)))
    pa, pb = _prompt_attention(p_bf, meta_pad, batch, seq, lam_p, a_g, b_g, lam_init)
    y_prompt = _finish(xp, pa, pb, wo_bf, g2, wq_bf, sk_bf, u_bf, v_bf, g_final)

    ca_k = cache_a_k[0].reshape(dec_batch, n_past * A_HEADS, 2 * HEAD_DIM)
    ca_v = cache_a_v[0].reshape(dec_batch, n_past * A_HEADS, 2 * HEAD_DIM)
    feature_major = lambda c: jnp.transpose(c[0], (0, 2, 3, 1)).reshape(dec_batch, B_WIDTH, n_past)
    tail_rows = lambda c: c[0, :, n_past - N_META:].reshape(dec_batch, N_META, B_WIDTH)
    sa, sb = _sample_attention(s_bf, ca_k, ca_v, feature_major(cache_b_k), feature_major(cache_b_v),
                               tail_rows(cache_b_k), tail_rows(cache_b_v), dec_batch, dec_seq,
                               lam_p, a_g, b_g, lam_init)
    y_sample = _finish(xs, sa, sb, wo_bf, g2, wq_bf, sk_bf, u_bf, v_bf, g_final)

    def prompt_cache_a(meta_rows, rows):
        full = rows.reshape(batch, (N_META + seq) * A_HEADS, 2 * HEAD_DIM)
        full = lax.dynamic_update_slice(full, jnp.broadcast_to(meta_rows[None], (batch,) + meta_rows.shape), (0, 0, 0))
        return full.reshape(1, batch, N_META + seq, A_HEADS, 2 * HEAD_DIM)

    def prompt_cache_b(meta_rows, frames_fm):
        m = jnp.broadcast_to(meta_rows.T[None], (batch, B_WIDTH, N_META))
        full = jnp.concatenate([m, frames_fm], axis=2).reshape(batch, B_HEADS, HEAD_DIM, N_META + seq)
        return jnp.transpose(full, (0, 3, 1, 2))[None]

    sample_cache = lambda rows, heads: rows.reshape(1, dec_batch, dec_seq, heads, A_WIDTH // heads)
    return (y_prompt.reshape(batch, seq, D_MODEL), y_sample.reshape(dec_batch, dec_seq, D_MODEL),
            prompt_cache_a(m_ak, p_ak), prompt_cache_a(m_av, p_av),
            prompt_cache_b(m_bk, p_bk), prompt_cache_b(m_bv, p_bv),
            sample_cache(s_ak, A_HEADS), sample_cache(s_av, A_HEADS),
            sample_cache(s_bk, B_HEADS), sample_cache(s_bv, B_HEADS))
```

```python
import functools
import math

import jax
import jax.numpy as jnp
from jax import lax
from jax.experimental import pallas as pl
from jax.experimental.pallas import tpu as pltpu

F32 = jnp.float32
BF16 = jnp.bfloat16

D_MODEL = 1024
CHUNK = 64
N_META = 16
HEAD_DIM = 64
A_HEADS = 4
B_HEADS = 8
A_WIDTH = A_HEADS * 2 * HEAD_DIM
B_WIDTH = B_HEADS * HEAD_DIM
MIX_WIDTH = A_WIDTH + B_WIDTH
QKV_WIDTH = 3 * MIX_WIDTH
ROT_DIM = HEAD_DIM // 4
ROPE_THETA = 500000.0
N_KEYS = 128
N_EXPERTS = N_KEYS * N_KEYS
PEER_HEADS = 8
PEER_TOPK = 16
D_KEY = 256
EPS = 1e-6
NEG_BIG = -1e30

LANES = 128
ROW_TILE = 512
POST_TILE = 512
ATT_BLOCK = 256
TOPK_TOKENS = 128
PEER_TOKENS = 256
PEER_CHUNK = 2048
PEER_SUB = 2048
PEER_BUILD_UNROLL = 64
W_PITCH = N_KEYS + 8
STICK_CHAINS = 4
DIFF_CHAINS = 1
STICK_EXIT = -110.0
VMEM_LIMIT = 56 * 1024 * 1024

COL_AQ, COL_AK, COL_AV = 0, 4, 8
COL_BQ, COL_BK, COL_BV = 12, 16, 20


def _dot(a, b):
    return jnp.dot(a, b, preferred_element_type=F32)


def _dot_nt(a, b):
    return lax.dot_general(a, b, (((1,), (1,)), ((), ())), preferred_element_type=F32)


def _dot_tn(a, b):
    return lax.dot_general(a, b, (((0,), (0,)), ((), ())), preferred_element_type=F32)


def _params(sem):
    return pltpu.CompilerParams(dimension_semantics=sem, vmem_limit_bytes=VMEM_LIMIT)


def _qkv_body(x_ref, g_ref, w_ref, c_ref, s1_ref, s2_ref, ak_ref, av_ref, bk_ref, bv_ref, bf_ref, *, b_feature_major):
    x = x_ref[...]
    xn = (x * lax.rsqrt(jnp.mean(x * x, axis=-1, keepdims=True) + EPS) * g_ref[...]).astype(BF16)
    cos_t = c_ref[...]
    sin_lo = s1_ref[...]
    sin_hi = s2_ref[...]
    f32_outs = {1: ak_ref, 2: av_ref, 4: bk_ref, 5: bv_ref}
    for grp in range(6):
        for half in range(2):
            y2 = _dot(xn, w_ref[:, (grp * 4 + half * 2) * LANES:(grp * 4 + half * 2 + 2) * LANES])
            for k in range(2):
                j = half * 2 + k
                cb = grp * 4 + j
                y = y2[:, k * LANES:(k + 1) * LANES]
                if grp in (0, 1):
                    y = (y * cos_t + pltpu.roll(y, LANES - ROT_DIM // 2, 1) * sin_lo
                         + pltpu.roll(y, ROT_DIM // 2, 1) * sin_hi)
                if grp in (1, 2):
                    f32_outs[grp][pl.ds(j, x.shape[0], stride=A_HEADS), :] = y
                elif grp in f32_outs and b_feature_major:
                    f32_outs[grp][j * LANES:(j + 1) * LANES, :] = y.T
                elif grp in f32_outs:
                    f32_outs[grp][:, j * LANES:(j + 1) * LANES] = y
                if grp in (0, 3):
                    y = y * (HEAD_DIM ** -0.5)
                bf_ref[:, cb * LANES:(cb + 1) * LANES] = y.astype(BF16)


def _rope_tables(pos):
    half = ROT_DIM // 2
    inv_freq = ROPE_THETA ** (-jnp.arange(half, dtype=F32) * 2.0 / ROT_DIM)
    ang = pos.astype(F32)[:, None] * inv_freq[None, :]
    cos, sin = jnp.cos(ang), jnp.sin(ang)
    t = pos.shape[0]
    pad = jnp.zeros((t, HEAD_DIM - ROT_DIM), F32)
    zero = jnp.zeros((t, half), F32)
    cos_t = jnp.concatenate([cos, cos, pad + 1.0], axis=1)
    sin_lo = jnp.concatenate([-sin, zero, pad], axis=1)
    sin_hi = jnp.concatenate([zero, sin, pad], axis=1)
    tile = lambda a: jnp.concatenate([a, a], axis=1)
    return tile(cos_t), tile(sin_lo), tile(sin_hi)


def _qkv_project(x2d, pos, g, w_bf, tm, lead=0, b_feature_major=False):
    n = x2d.shape[0]
    t = pos.shape[0]
    per = t // tm
    tabs = _rope_tables(pos)
    row = lambda b, j: (b * per + j, 0)
    tab = lambda b, j: (j, 0)
    full = lambda b, j: (0, 0)
    f32_out = jax.ShapeDtypeStruct((n, A_WIDTH), F32)
    a_rows = tm * A_HEADS
    a_out = jax.ShapeDtypeStruct(((n // t) * (lead + t) * A_HEADS, LANES), F32)
    a_spec = pl.BlockSpec((pl.Element(a_rows), pl.Element(LANES)),
                          lambda b, j: (pl.multiple_of(b * ((lead + t) * A_HEADS) + lead * A_HEADS + j * a_rows, 8),
                                        0))
    if b_feature_major:
        b_out = jax.ShapeDtypeStruct((n // t, B_WIDTH, t), F32)
        b_spec = pl.BlockSpec((None, B_WIDTH, tm), lambda b, j: (b, 0, j))
    else:
        b_out, b_spec = f32_out, pl.BlockSpec((tm, B_WIDTH), row)
    return pl.pallas_call(
        functools.partial(_qkv_body, b_feature_major=b_feature_major),
        grid=(n // t, per),
        in_specs=[pl.BlockSpec((tm, D_MODEL), row), pl.BlockSpec((1, D_MODEL), full),
                  pl.BlockSpec((D_MODEL, QKV_WIDTH), full),
                  pl.BlockSpec((tm, LANES), tab), pl.BlockSpec((tm, LANES), tab), pl.BlockSpec((tm, LANES), tab)],
        out_specs=[a_spec, a_spec, b_spec, b_spec, pl.BlockSpec((tm, QKV_WIDTH), row)],
        out_shape=[a_out, a_out, b_out, b_out, jax.ShapeDtypeStruct((n, QKV_WIDTH), BF16)],
        compiler_params=_params(("parallel", "parallel")),
        name="qkv",
    )(x2d, g.reshape(1, D_MODEL), w_bf, *tabs)


def _stack_heads_t(q):
    qt = q.astype(F32).T
    sub = lax.broadcasted_iota(jnp.int32, qt.shape, 0)
    first = jnp.where(sub < HEAD_DIM, qt, 0.0)
    second = jnp.where(sub >= HEAD_DIM, qt, 0.0)
    return jnp.concatenate([first, second], axis=1).astype(BF16)


def _softmax_init(n2):
    return jnp.full((1, n2), NEG_BIG, F32), jnp.zeros((1, n2), F32)


def _softmax_update(s, m, l):
    m_new = jnp.maximum(m, jnp.max(s, axis=0, keepdims=True))
    alpha = jnp.exp(m - m_new)
    p = jnp.exp(s - m_new)
    return m_new, alpha, alpha * l + jnp.sum(p, axis=0, keepdims=True), p.astype(BF16)


def _softmax_first(kbs, vbs, mask, chains):
    state = []
    for kb, vb, (qst_sc, _, _, acc_sc) in zip(kbs, vbs, chains):
        m, l = _softmax_init(qst_sc.shape[1])
        sc = jnp.where(mask, _dot(kb, qst_sc[...]), NEG_BIG)
        m, _, l, p = _softmax_update(sc, m, l)
        acc_sc[...] = _dot_tn(vb, p)
        state.append((m, l))
    return tuple(state)


def _softmax_scratch(tk, n2):
    return [pltpu.VMEM((LANES, n2), BF16), pltpu.VMEM((tk, n2), F32), pltpu.VMEM((tk, n2), BF16),
            pltpu.VMEM((LANES, n2), F32)] * DIFF_CHAINS


def _chain_scratch(scratch):
    return [scratch[4 * ch:4 * ch + 4] for ch in range(len(scratch) // 4)]


def _softmax_scan(n_blocks, kb_fn, vb_fn, last_mask, state, chains):
    n2 = chains[0][0].shape[1]
    mm_cols = min(2 * LANES, n2)

    def block_step(state, v_prevs, k_nexts, mask):
        stats = [([], []) for _ in chains]
        for c0 in range(0, n2, mm_cols):
            wide = slice(c0, c0 + mm_cols)
            pvs = [_dot_tn(v_prev, p_sc[:, wide]) for v_prev, (_, _, p_sc, _) in zip(v_prevs, chains)]
            tiles = []
            for ch, (_, s_sc, _, _) in enumerate(chains):
                for t0 in range(c0, c0 + mm_cols, LANES):
                    cols = slice(t0, t0 + LANES)
                    sc = s_sc[:, cols]
                    if mask is not None:
                        sc = jnp.where(mask[:, cols], sc, NEG_BIG)
                    tiles.append((ch, cols, sc))
            if k_nexts is not None:
                for k_next, (qst_sc, s_sc, _, _) in zip(k_nexts, chains):
                    s_sc[:, wide] = _dot(k_next, qst_sc[:, wide])
            for ch, cols, sc in tiles:
                _, _, p_sc, acc_sc = chains[ch]
                m, l = state[ch]
                m_new, alpha, l_new, p = _softmax_update(sc, m[:, cols], l[:, cols])
                p_sc[:, cols] = p
                off = cols.start - c0
                acc_sc[:, cols] = alpha * (acc_sc[:, cols] + pvs[ch][:, off:off + LANES])
                stats[ch][0].append(m_new)
                stats[ch][1].append(l_new)
        return tuple((jnp.concatenate(ms, axis=1), jnp.concatenate(ls, axis=1)) for ms, ls in stats)

    for kb, (qst_sc, s_sc, p_sc, _) in zip(kb_fn(0), chains):
        s_sc[...] = _dot(kb, qst_sc[...])
        p_sc[...] = jnp.zeros(p_sc.shape, BF16)
    last = n_blocks - 1

    def body(j, carry):
        return block_step(carry, vb_fn(jnp.maximum(j - 1, 0)), kb_fn(j + 1), None)

    state = lax.fori_loop(0, last // 2, lambda jj, carry: body(2 * jj + 1, body(2 * jj, carry)), state)
    state = lax.cond(last % 2 == 1, lambda carry: body(last - 1, carry), lambda carry: carry, state)
    state = block_step(state, vb_fn(jnp.maximum(last - 1, 0)), None, last_mask)
    for vb, (_, _, p_sc, acc_sc) in zip(vb_fn(last), chains):
        acc_sc[...] += _dot_tn(vb, p_sc[...])
    return state


def _diff_finish(state, chains, lam_ref, g_ref, o_ref, lam_init):
    for ch, ((_, l), (_, _, _, acc_sc)) in enumerate(zip(state, chains)):
        _diff_finish_one(l, acc_sc, lam_ref, g_ref, o_ref.at[:, ch * LANES:(ch + 1) * LANES], lam_init)


def _diff_finish_one(l, acc_sc, lam_ref, g_ref, o_ref, lam_init):
    tq = o_ref.shape[0]
    acc = acc_sc[...]
    lp = lam_ref[...]
    lam = (jnp.exp(jnp.sum(lp[0:1] * lp[1:2], axis=1, keepdims=True))
           - jnp.exp(jnp.sum(lp[2:3] * lp[3:4], axis=1, keepdims=True)) + lam_init)
    on = (acc / l).T
    o = on[:tq] - lam * on[tq:]
    o = o * lax.rsqrt(jnp.mean(o * o, axis=1, keepdims=True) + EPS) * g_ref[...] * (1.0 - lam_init)
    o_ref[...] = o.astype(o_ref.dtype)


def _stick_init(n2):
    return jnp.zeros((1, n2), F32), jnp.zeros((LANES, n2), F32)


def _stick_step(qsts, kbs, vbs, tri, mask_fn, state, kv_transposed=False):
    zs = [(_dot_tn(kb, q) if kv_transposed else _dot(kb, q)) for q, kb in zip(qsts, kbs)]
    mask = None
    if mask_fn is not None:
        mask = mask_fn(lax.broadcasted_iota(jnp.int32, zs[0].shape, 0),
                       lax.broadcasted_iota(jnp.int32, zs[0].shape, 1))
    cums = []
    for z in zs:
        lm = -(jnp.maximum(z, 0.0) + jnp.log(1.0 + jnp.exp(-jnp.abs(z))))
        if mask is not None:
            lm = jnp.where(mask, lm, 0.0)
        hi = lm.astype(BF16)
        lo = (lm - hi.astype(F32)).astype(BF16)
        cums.append(_dot(tri, hi) + _dot(tri, lo))
    out = []
    for z, cum, vb, (c, acc) in zip(zs, cums, vbs, state):
        a = jnp.exp(z + cum + c)
        if mask is not None:
            a = jnp.where(mask, a, 0.0)
        a = a.astype(BF16)
        out.append((c + cum[0:1, :], acc + (_dot(vb, a) if kv_transposed else _dot_tn(vb, a))))
    return tuple(out)


def _stick_cmax(state):
    return functools.reduce(jnp.maximum, [jnp.max(c) for c, _ in state])


def _tri(n):
    r = lax.broadcasted_iota(jnp.int32, (n, n), 0)
    c = lax.broadcasted_iota(jnp.int32, (n, n), 1)
    return jnp.where(c >= r, 1.0, 0.0).astype(BF16)


def _stick_finish(state, g_ref, o_ref):
    tq = o_ref.shape[0]
    lane = lax.broadcasted_iota(jnp.int32, (tq, LANES), 1)
    first = lane < HEAD_DIM
    for ch, (_, acc) in enumerate(state):
        acc = acc.T
        o = jnp.where(first, acc[:tq], acc[tq:])
        sq = o * o
        ss0 = jnp.sum(jnp.where(first, sq, 0.0), axis=1, keepdims=True)
        ss1 = jnp.sum(jnp.where(first, 0.0, sq), axis=1, keepdims=True)
        ms = jnp.where(first, ss0, ss1) * (1.0 / HEAD_DIM)
        o_ref[:, ch * LANES:(ch + 1) * LANES] = (o * lax.rsqrt(ms + EPS) * g_ref[...]).astype(o_ref.dtype)


def _stick_scan(n_blocks, step_fn, state):
    def cond(st):
        return jnp.logical_and(st[0] < n_blocks, st[1] > STICK_EXIT)

    def body(st):
        state = step_fn(st[0], st[2])
        return st[0] + 1, _stick_cmax(state), state

    out = lax.while_loop(cond, body, (jnp.int32(0), _stick_cmax(state), state))
    return out[1], out[2]


def _diff_prompt_body(q_ref, k_ref, v_ref, km_ref, vm_ref, lam_ref, g_ref, o_ref, *scratch, lam_init):
    tq = q_ref.shape[0]
    i = pl.program_id(2)
    chains = _chain_scratch(scratch)
    heads = range(len(chains))
    part = lambda x, ch: x[:, ch * LANES:(ch + 1) * LANES]
    for ch in heads:
        chains[ch][0][...] = _stack_heads_t(part(q_ref[...], ch))
    nm = km_ref.shape[0]
    key = lax.broadcasted_iota(jnp.int32, (nm, 2 * tq), 0)
    meta = lambda ref: [part(ref[...], ch) for ch in heads]
    state = _softmax_first(meta(km_ref), meta(vm_ref), key < N_META, chains)
    key = lax.broadcasted_iota(jnp.int32, (tq, 2 * tq), 0)
    query = lax.broadcasted_iota(jnp.int32, (tq, 2 * tq), 1) % tq
    block = lambda ref: (lambda j: [ref[pl.ds(pl.multiple_of(j * tq, tq), tq), ch * LANES:(ch + 1) * LANES]
                                    for ch in heads])
    state = _softmax_scan(i + 1, block(k_ref), block(v_ref), (key // CHUNK) <= (query // CHUNK), state, chains)
    _diff_finish(state, chains, lam_ref, g_ref, o_ref, lam_init)


def _stick_prompt_body(q_ref, k_ref, v_ref, km_ref, vm_ref, g_ref, o_ref):
    tq = q_ref.shape[0]
    i = pl.program_id(2)
    chains = range(q_ref.shape[1] // LANES)
    part = lambda x, ch: x[:, ch * LANES:(ch + 1) * LANES]
    qsts = [_stack_heads_t(part(q_ref[...], ch)) for ch in chains]
    tri = _tri(tq)
    blocks = lambda ref, j: [ref[pl.ds(pl.multiple_of(j * tq, tq), tq), ch * LANES:(ch + 1) * LANES] for ch in chains]
    state = _stick_step(qsts, blocks(k_ref, i), blocks(v_ref, i), tri, lambda key, col: key < col % tq,
                        tuple(_stick_init(2 * tq) for _ in chains))
    cmax, state = _stick_scan(
        i, lambda n, st: _stick_step(qsts, blocks(k_ref, i - 1 - n), blocks(v_ref, i - 1 - n), tri, None, st), state)
    nm = km_ref.shape[0]
    meta = lambda ref: [part(ref[...], ch) for ch in chains]
    state = lax.cond(cmax > STICK_EXIT,
                     lambda st: _stick_step(qsts, meta(km_ref), meta(vm_ref), tri[:nm, :nm],
                                            lambda key, col: key < N_META, st),
                     lambda st: st, state)
    _stick_finish(state, g_ref, o_ref)


def _prompt_attention(qkv_bf, meta_bf, batch, seq, diff_lambda, a_g, b_g, lam_init):
    tq = ATT_BLOCK
    nq = seq // tq
    n = batch * seq
    full = lambda b, g, i: (0, 0)
    nm = meta_bf.shape[0]

    def specs(cq, ck, cv, chains):
        w = chains * LANES
        col = lambda off: off // chains
        return ([pl.BlockSpec((tq, w), lambda b, g, i: (b * nq + i, col(cq) + g)),
                 pl.BlockSpec((seq, w), lambda b, g, i: (b, col(ck) + g)),
                 pl.BlockSpec((seq, w), lambda b, g, i: (b, col(cv) + g)),
                 pl.BlockSpec((nm, w), lambda b, g, i: (0, col(ck) + g)),
                 pl.BlockSpec((nm, w), lambda b, g, i: (0, col(cv) + g))],
                pl.BlockSpec((tq, w), lambda b, g, i: (b * nq + i, g)))

    out_shape = jax.ShapeDtypeStruct((n, A_WIDTH), BF16)
    sem = ("parallel", "parallel", "arbitrary")
    a_in, a_out_spec = specs(COL_AQ, COL_AK, COL_AV, DIFF_CHAINS)
    a_out = pl.pallas_call(
        functools.partial(_diff_prompt_body, lam_init=lam_init),
        grid=(batch, A_HEADS // DIFF_CHAINS, nq),
        in_specs=a_in + [pl.BlockSpec((4, HEAD_DIM), full), pl.BlockSpec((1, LANES), full)],
        out_specs=a_out_spec, out_shape=out_shape, scratch_shapes=_softmax_scratch(tq, 2 * tq),
        compiler_params=_params(sem), name="diff_prompt",
    )(qkv_bf, qkv_bf, qkv_bf, meta_bf, meta_bf, diff_lambda, a_g.reshape(1, LANES))
    b_g2 = jnp.concatenate([b_g, b_g]).reshape(1, LANES)
    b_in, b_out_spec = specs(COL_BQ, COL_BK, COL_BV, STICK_CHAINS)
    b_out = pl.pallas_call(
        _stick_prompt_body,
        grid=(batch, B_HEADS // 2 // STICK_CHAINS, nq),
        in_specs=b_in + [pl.BlockSpec((1, LANES), full)],
        out_specs=b_out_spec, out_shape=out_shape,
        compiler_params=_params(sem), name="stick_prompt",
    )(qkv_bf, qkv_bf, qkv_bf, meta_bf, meta_bf, b_g2)
    return a_out, b_out


def _last_block(tail_ref, new_ref):
    tail = tail_ref[...].astype(BF16)
    new = new_ref[...]
    pad = jnp.zeros((LANES - tail.shape[0] - new.shape[0], LANES), BF16)
    return jnp.concatenate([tail, new, pad], axis=0)


def _diff_sample_body(q_ref, kn_ref, vn_ref, kc_ref, vc_ref, kt_ref, vt_ref, lam_ref, g_ref, o_ref, *scratch,
                      lam_init):
    ts = q_ref.shape[0]
    chains = _chain_scratch(scratch)
    heads = range(len(chains))
    h0 = pl.program_id(1) * len(chains)
    nt = kt_ref.shape[0] // A_HEADS
    part = lambda x, ch: x[:, ch * LANES:(ch + 1) * LANES]
    for ch in heads:
        chains[ch][0][...] = _stack_heads_t(part(q_ref[...], ch))
    key = lax.broadcasted_iota(jnp.int32, (LANES, 2 * ts), 0)
    head_rows = lambda ref, ch, first, n: ref[pl.ds(first * A_HEADS + h0 + ch, n, stride=A_HEADS), :].astype(BF16)
    pad = jnp.zeros((LANES - nt - ts, LANES), BF16)
    last = lambda tail_ref, new_ref: [
        jnp.concatenate([head_rows(tail_ref, ch, 0, nt), part(new_ref[...], ch), pad], axis=0) for ch in heads]
    state = _softmax_first(last(kt_ref, kn_ref), last(vt_ref, vn_ref), key < nt + ts, chains)
    blk = ATT_BLOCK
    block = lambda ref: (lambda j: [head_rows(ref, ch, j * blk, blk) for ch in heads])
    state = _softmax_scan(kc_ref.shape[0] // (A_HEADS * blk), block(kc_ref), block(vc_ref), None, state, chains)
    _diff_finish(state, chains, lam_ref, g_ref, o_ref, lam_init)


def _stick_sample_body(q_ref, kn_ref, vn_ref, kc_ref, vc_ref, kt_ref, vt_ref, g_ref, o_ref):
    ts = q_ref.shape[0]
    nt = kt_ref.shape[0]
    chains = range(q_ref.shape[1] // LANES)
    part = lambda x, ch: x[:, ch * LANES:(ch + 1) * LANES]
    qsts = [_stack_heads_t(part(q_ref[...], ch)) for ch in chains]
    blk = ATT_BLOCK
    tri = _tri(blk)
    pad = jnp.zeros((LANES - nt - ts, LANES), BF16)
    last = lambda tail_ref, new_ref: [
        jnp.concatenate([part(tail_ref[...], ch).astype(BF16), part(new_ref[...], ch), pad], axis=0) for ch in chains]
    last_mask = lambda key, col: (key < nt) | ((key < nt + ts) & (key - nt < col % ts))
    state = _stick_step(qsts, last(kt_ref, kn_ref), last(vt_ref, vn_ref), tri[:LANES, :LANES], last_mask,
                        tuple(_stick_init(2 * ts) for _ in chains))
    nb = kc_ref.shape[1] // blk
    blocks = lambda ref, j: [ref[ch * LANES:(ch + 1) * LANES, pl.ds(pl.multiple_of(j * blk, blk), blk)].astype(BF16)
                             for ch in chains]
    _, state = _stick_scan(
        nb, lambda n, st: _stick_step(qsts, blocks(kc_ref, nb - 1 - n), blocks(vc_ref, nb - 1 - n), tri, None, st,
                                      kv_transposed=True),
        state)
    _stick_finish(state, g_ref, o_ref)


def _sample_attention(qkv_bf, ca_k, ca_v, cbt_k, cbt_v, cb_tail_k, cb_tail_v, dec_batch, dec_seq,
                      diff_lambda, a_g, b_g, lam_init):
    n_past = cbt_k.shape[2]
    main = n_past - N_META
    tail_blk = main // N_META
    full = lambda b, g: (0, 0)

    def new_specs(cq, ck, cv, chains):
        new = lambda off: pl.BlockSpec((dec_seq, chains * LANES), lambda b, g: (b, off // chains + g))
        return [new(cq), new(ck), new(cv)], pl.BlockSpec((dec_seq, chains * LANES), lambda b, g: (b, g))

    w = STICK_CHAINS * LANES
    b_main = pl.BlockSpec((None, w, main), lambda b, g: (b, g, 0))
    b_tail = pl.BlockSpec((None, N_META, w), lambda b, g: (b, 0, g))
    a_main = pl.BlockSpec((None, main * A_HEADS, LANES), lambda b, g: (b, 0, 0))
    a_tail = pl.BlockSpec((None, N_META * A_HEADS, LANES), lambda b, g: (b, tail_blk, 0))
    out_shape = jax.ShapeDtypeStruct((dec_batch * dec_seq, A_WIDTH), BF16)
    sem = ("parallel", "parallel")
    a_new, a_out_spec = new_specs(COL_AQ, COL_AK, COL_AV, DIFF_CHAINS)
    a_out = pl.pallas_call(
        functools.partial(_diff_sample_body, lam_init=lam_init),
        grid=(dec_batch, A_HEADS // DIFF_CHAINS),
        in_specs=(a_new + [a_main, a_main, a_tail, a_tail]
                  + [pl.BlockSpec((4, HEAD_DIM), full), pl.BlockSpec((1, LANES), full)]),
        out_specs=a_out_spec, out_shape=out_shape, scratch_shapes=_softmax_scratch(ATT_BLOCK, 2 * dec_seq),
        compiler_params=_params(sem), name="diff_sample",
    )(qkv_bf, qkv_bf, qkv_bf, ca_k, ca_v, ca_k, ca_v, diff_lambda, a_g.reshape(1, LANES))
    b_g2 = jnp.concatenate([b_g, b_g]).reshape(1, LANES)
    b_new, b_out_spec = new_specs(COL_BQ, COL_BK, COL_BV, STICK_CHAINS)
    b_out = pl.pallas_call(
        _stick_sample_body,
        grid=(dec_batch, B_HEADS // 2 // STICK_CHAINS),
        in_specs=b_new + [b_main, b_main, b_tail, b_tail, pl.BlockSpec((1, LANES), full)],
        out_specs=b_out_spec, out_shape=out_shape,
        compiler_params=_params(sem), name="stick_sample",
    )(qkv_bf, qkv_bf, qkv_bf, cbt_k, cbt_v, cb_tail_k, cb_tail_v, b_g2)
    return a_out, b_out


def _post_body(x_ref, a_ref, b_ref, wo_ref, g_ref, wqt_ref, sk_ref, h_ref, hn_ref, s_ref):
    h = (x_ref[...] + _dot(a_ref[...], wo_ref[:A_WIDTH, :]) + _dot(b_ref[...], wo_ref[A_WIDTH:, :]))
    h_ref[...] = h
    hn = h * lax.rsqrt(jnp.mean(h * h, axis=-1, keepdims=True) + EPS) * g_ref[...]
    hn_ref[...] = hn.astype(BF16)
    hnt = hn.T.astype(BF16)
    half = D_KEY // 2
    for hd in range(PEER_HEADS):
        qt = _dot(wqt_ref[hd * D_KEY:(hd + 1) * D_KEY, :], hnt).astype(BF16)
        for p in range(2):
            s_ref[2 * hd + p] = _dot(sk_ref[p], qt[p * half:(p + 1) * half, :])


def _post_attention(x2d, a_out, b_out, wo_bf, g_ffn, wq_bf, sk_bf, tm):
    n = x2d.shape[0]
    row = lambda i: (i, 0)
    full = lambda i: (0, 0)
    return pl.pallas_call(
        _post_body,
        grid=(n // tm,),
        in_specs=[pl.BlockSpec((tm, D_MODEL), row), pl.BlockSpec((tm, A_WIDTH), row), pl.BlockSpec((tm, B_WIDTH), row),
                  pl.BlockSpec((MIX_WIDTH, D_MODEL), full), pl.BlockSpec((1, D_MODEL), full),
                  pl.BlockSpec((PEER_HEADS * D_KEY, D_MODEL), full),
                  pl.BlockSpec((2, N_KEYS, D_KEY // 2), lambda i: (0, 0, 0))],
        out_specs=[pl.BlockSpec((tm, D_MODEL), row), pl.BlockSpec((tm, D_MODEL), row),
                   pl.BlockSpec((2 * PEER_HEADS, N_KEYS, tm), lambda i: (0, 0, i))],
        out_shape=[jax.ShapeDtypeStruct((n, D_MODEL), F32), jax.ShapeDtypeStruct((n, D_MODEL), BF16),
                   jax.ShapeDtypeStruct((2 * PEER_HEADS, N_KEYS, n), F32)],
        compiler_params=_params(("parallel",)),
        name="post",
    )(x2d, a_out, b_out, wo_bf, g_ffn.reshape(1, D_MODEL), wq_bf, sk_bf)


def _extract_top(s, iota, take, val_ref, aux, aux_ref):
    n_rows = float(s.shape[0])
    for r in range(take):
        m = jnp.max(s, axis=0, keepdims=True)
        pos = jnp.min(jnp.where(s == m, iota, n_rows), axis=0, keepdims=True)
        sel = iota == pos
        val_ref[r:r + 1, :] = m
        if aux is None:
            aux_ref[r:r + 1, :] = pos
        else:
            aux_ref[r:r + 1, :] = jnp.sum(jnp.where(sel, aux, 0.0), axis=0, keepdims=True)
        s = jnp.where(sel, -jnp.inf, s)


def _topk_body(s_ref, row_ref, col_ref, gate_ref, v0_ref, i0_ref, v1_ref, i1_ref, bs_ref, id_ref,
               rows_sc, cols_sc, gates_sc):
    tt = s_ref.shape[2]
    k = PEER_TOPK
    key_iota = lax.broadcasted_iota(jnp.int32, (N_KEYS, tt), 0).astype(F32)
    n_cand = k + 8 * 7 + 8
    cand_iota = lax.broadcasted_iota(jnp.int32, (n_cand, tt), 0).astype(F32)

    def head(h, carry):
        _extract_top(s_ref[2 * h], key_iota, k, v0_ref, None, i0_ref)
        _extract_top(s_ref[2 * h + 1], key_iota, k, v1_ref, None, i1_ref)
        v1_all, i1_all = v1_ref[...], i1_ref[...]
        cs = [v0_ref[0:1, :] + v1_all]
        ci = [i0_ref[0:1, :] * N_KEYS + i1_all]
        for r in range(1, 8):
            cs.append(v0_ref[r:r + 1, :] + v1_all[:8])
            ci.append(i0_ref[r:r + 1, :] * N_KEYS + i1_all[:8])
        cs.append(v0_ref[8:16, :] + v1_all[0:1])
        ci.append(i0_ref[8:16, :] * N_KEYS + i1_all[0:1])
        _extract_top(jnp.concatenate(cs, axis=0), cand_iota, k, bs_ref, jnp.concatenate(ci, axis=0), id_ref)
        best = bs_ref[...]
        e = jnp.exp(best - best[0:1])
        ids = id_ref[...]
        rows = jnp.floor(ids * (1.0 / N_KEYS))
        dst = pl.ds(pl.multiple_of(h * k, k), k)
        gates_sc[dst, :] = e / jnp.sum(e, axis=0, keepdims=True)
        rows_sc[dst, :] = rows
        cols_sc[dst, :] = ids - rows * N_KEYS
        return carry

    lax.fori_loop(0, PEER_HEADS, head, 0, unroll=4)
    row_ref[...] = rows_sc[...].T
    col_ref[...] = cols_sc[...].T
    gate_ref[...] = gates_sc[...].T


def _peer_topk(scores):
    n = scores.shape[2]
    tt = TOPK_TOKENS
    slots = PEER_HEADS * PEER_TOPK
    out = jax.ShapeDtypeStruct((n, slots), F32)
    return pl.pallas_call(
        _topk_body,
        grid=(n // tt,),
        in_specs=[pl.BlockSpec((2 * PEER_HEADS, N_KEYS, tt), lambda i: (0, 0, i))],
        out_specs=[pl.BlockSpec((tt, slots), lambda i: (i, 0))] * 3,
        out_shape=[out] * 3,
        scratch_shapes=[pltpu.VMEM((PEER_TOPK, tt), F32)] * 6 + [pltpu.VMEM((slots, tt), F32)] * 3,
        compiler_params=_params(("parallel",)),
        name="topk",
    )(scores)


def _gelu_x2(x):
    return x * (1.0 + lax.erf(x * (2.0 ** -0.5)))


def _peer_body(hn_ref, row_ref, col_ref, gate_ref, ut_ref, v_ref, h_ref, gf_ref, y_ref, w_sc, acc_sc):
    tb = hn_ref.shape[0]
    c = pl.program_id(1)
    per_chunk = v_ref.shape[0] // N_KEYS

    @pl.when(c == 0)
    def _():
        acc_sc[...] = jnp.zeros_like(acc_sc)
        sub = lax.broadcasted_iota(jnp.int32, (N_KEYS, LANES), 0).astype(F32)

        def build(t, carry):
            r = row_ref[pl.ds(t, 1), :]
            cc = col_ref[pl.ds(t, 1), :]
            g = gate_ref[pl.ds(t, 1), :] * 0.5
            at = jnp.where(sub == r, g, 0.0).astype(BF16)
            bt = jnp.where(sub == cc, 1.0, 0.0).astype(BF16)
            w_sc[pl.ds(pl.multiple_of(t * W_PITCH, 8), N_KEYS), :] = _dot_nt(at, bt)
            return carry

        lax.fori_loop(0, tb, build, 0, unroll=PEER_BUILD_UNROLL)

    x = hn_ref[...]
    total = None
    for s in range(v_ref.shape[0] // PEER_SUB):
        experts = slice(s * PEER_SUB, (s + 1) * PEER_SUB)
        pre = _dot(x, ut_ref[:, experts])
        first = c * per_chunk + s * (PEER_SUB // N_KEYS)
        w = jnp.concatenate([w_sc[pl.ds(first + a, tb, stride=W_PITCH), :] for a in range(PEER_SUB // N_KEYS)],
                            axis=1)
        part = _dot((_gelu_x2(pre) * w).astype(BF16), v_ref[experts, :])
        total = part if total is None else total + part
    acc_sc[...] += total

    @pl.when(c == pl.num_programs(1) - 1)
    def _():
        h = h_ref[...] + acc_sc[...]
        y_ref[...] = h * lax.rsqrt(jnp.mean(h * h, axis=-1, keepdims=True) + EPS) * gf_ref[...]


def _peer_dense(hn, rows, cols, gates, ut_bf, v_bf, h, g_final):
    n = hn.shape[0]
    tb = PEER_TOKENS
    slots = PEER_HEADS * PEER_TOPK
    tok = lambda i, c: (i, 0)
    chunk = lambda i, c: (c, 0)
    return pl.pallas_call(
        _peer_body,
        grid=(n // tb, N_EXPERTS // PEER_CHUNK),
        in_specs=[pl.BlockSpec((tb, D_MODEL), tok), pl.BlockSpec((tb, slots), tok), pl.BlockSpec((tb, slots), tok),
                  pl.BlockSpec((tb, slots), tok), pl.BlockSpec((D_MODEL, PEER_CHUNK), lambda i, c: (0, c)),
                  pl.BlockSpec((PEER_CHUNK, D_MODEL), chunk), pl.BlockSpec((tb, D_MODEL), tok),
                  pl.BlockSpec((1, D_MODEL), lambda i, c: (0, 0))],
        out_specs=pl.BlockSpec((tb, D_MODEL), tok),
        out_shape=jax.ShapeDtypeStruct((n, D_MODEL), F32),
        scratch_shapes=[pltpu.VMEM((tb * W_PITCH, LANES), F32), pltpu.VMEM((tb, D_MODEL), F32)],
        compiler_params=_params(("parallel", "arbitrary")),
        name="peer",
    )(hn, rows, cols, gates, ut_bf, v_bf, h, g_final.reshape(1, D_MODEL))


def _finish(x2d, a_out, b_out, wo_bf, g_ffn, wq_bf, sk_bf, u_bf, v_bf, g_final):
    h, hn, scores = _post_attention(x2d, a_out, b_out, wo_bf, g_ffn, wq_bf, sk_bf, min(POST_TILE, x2d.shape[0]))
    rows, cols, gates = _peer_topk(scores)
    return _peer_dense(hn, rows, cols, gates, u_bf, v_bf, h, g_final)


def kernel(x_prompt, x_sample, cache_a_k, cache_a_v, cache_b_k, cache_b_v, meta_tokens, g_attn, w_qkv,
           diff_lambda, a_norm_g, b_norm_g, w_o, g_ffn, w_peer_q, peer_sub_keys, peer_u, peer_v, g_final):
    batch, seq, _ = x_prompt.shape
    dec_batch, dec_seq, _ = x_sample.shape
    depth = w_qkv.shape[0]
    assert depth == 1, "single-layer step"
    n_past = cache_a_k.shape[2]
    lam_init = 0.8 - 0.6 * math.exp(-0.3 * 0)

    w_bf = w_qkv[0].astype(BF16)
    wo_bf = w_o[0].astype(BF16)
    wq_bf = w_peer_q[0].astype(BF16).T
    sk_bf = peer_sub_keys[0].astype(BF16)
    u_bf = peer_u[0].astype(BF16).T
    v_bf = peer_v[0].astype(BF16)
    g1, g2 = g_attn[0], g_ffn[0]
    lam_p, a_g, b_g = diff_lambda[0], a_norm_g[0], b_norm_g[0]

    xp = x_prompt.reshape(batch * seq, D_MODEL)
    xs = x_sample.reshape(dec_batch * dec_seq, D_MODEL)
    p_ak, p_av, p_bk, p_bv, p_bf = _qkv_project(xp, N_META + jnp.arange(seq, dtype=jnp.int32), g1, w_bf, ROW_TILE,
                                                lead=N_META, b_feature_major=True)
    m_ak, m_av, m_bk, m_bv, m_bf = _qkv_project(meta_tokens.astype(F32), jnp.arange(N_META, dtype=jnp.int32),
                                                g1, w_bf, N_META)
    s_ak, s_av, s_bk, s_bv, s_bf = _qkv_project(xs, n_past + jnp.arange(dec_seq, dtype=jnp.int32), g1, w_bf, dec_seq)

    meta_pad = jnp.pad(m_bf, ((0, LANES - N_META), (0, 0)))
    pa, pb = _prompt_attention(p_bf, meta_pad, batch, seq, lam_p, a_g, b_g, lam_init)
    y_prompt = _finish(xp, pa, pb, wo_bf, g2, wq_bf, sk_bf, u_bf, v_bf, g_final)

    ca_k = cache_a_k[0].reshape(dec_batch, n_past * A_HEADS, 2 * HEAD_DIM)
    ca_v = cache_a_v[0].reshape(dec_batch, n_past * A_HEADS, 2 * HEAD_DIM)
    feature_major = lambda c: jnp.transpose(c[0], (0, 2, 3, 1)).reshape(dec_batch, B_WIDTH, n_past)
    tail_rows = lambda c: c[0, :, n_past - N_META:].reshape(dec_batch, N_META, B_WIDTH)
    sa, sb = _sample_attention(s_bf, ca_k, ca_v, feature_major(cache_b_k), feature_major(cache_b_v),
                               tail_rows(cache_b_k), tail_rows(cache_b_v), dec_batch, dec_seq,
                               lam_p, a_g, b_g, lam_init)
    y_sample = _finish(xs, sa, sb, wo_bf, g2, wq_bf, sk_bf, u_bf, v_bf, g_final)

    def prompt_cache_a(meta_rows, rows):
        full = rows.reshape(batch, (N_META + seq) * A_HEADS, 2 * HEAD_DIM)
        full = lax.dynamic_update_slice(full, jnp.broadcast_to(meta_rows[None], (batch,) + meta_rows.shape), (0, 0, 0))
        return full.reshape(1, batch, N_META + seq, A_HEADS, 2 * HEAD_DIM)

    def prompt_cache_b(meta_rows, frames_fm):
        m = jnp.broadcast_to(meta_rows.T[None], (batch, B_WIDTH, N_META))
        full = jnp.concatenate([m, frames_fm], axis=2).reshape(batch, B_HEADS, HEAD_DIM, N_META + seq)
        return jnp.transpose(full, (0, 3, 1, 2))[None]

    sample_cache = lambda rows, heads: rows.reshape(1, dec_batch, dec_seq, heads, A_WIDTH // heads)
    return (y_prompt.reshape(batch, seq, D_MODEL), y_sample.reshape(dec_batch, dec_seq, D_MODEL),
            prompt_cache_a(m_ak, p_ak), prompt_cache_a(m_av, p_av),
            prompt_cache_b(m_bk, p_bk), prompt_cache_b(m_bv, p_bv),
            sample_cache(s_ak, A_HEADS), sample_cache(s_av, A_HEADS),
            sample_cache(s_bk, B_HEADS), sample_cache(s_bv, B_HEADS))
```
